```python
import math
import jax, jax.numpy as jnp
from jax import lax
import numpy as np

D_MODEL = 2048
BATCH = 16
SEQ = 2048
DEPTH = 2
DEC_BATCH = 8
DEC_SEQ = 4096
PAST_LEN = 128

N_BRANCH = 4
W_BR = D_MODEL // N_BRANCH
D_MIX = N_BRANCH * W_BR
H_GDN = 4
DK_GDN = W_BR // H_GDN
GDN_CONV = 5
GDN_CHUNK = 64
H_DIFF = 4
DV_DIFF = W_BR // H_DIFF
DQK_DIFF = DV_DIFF // 2
ROT_DIM = DQK_DIFF // 4
ROPE_THETA = 500000.0
Q_BLOCK = 128
CONV_W = 3
H_MEM = 4
D_MEM_HEAD = W_BR // H_MEM
N_MEM = 256
EPS = 1e-6

IN_SPLITS = (3 * W_BR, 2 * H_GDN, 2 * H_GDN, W_BR,
             W_BR, W_BR, W_BR, W_BR,
             W_BR, W_BR, W_BR, W_BR,
             W_BR, W_BR)
IN_COLS = sum(IN_SPLITS)
IN_SPLIT_IDX = tuple(int(i) for i in np.cumsum(IN_SPLITS)[:-1])

kernel_name = 'hybrid_gdn_diffattn_shortconv_memory_encoder'


def _rmsnorm(x, w):
    xf = x.astype(jnp.float32)
    y = xf * lax.rsqrt(jnp.mean(xf * xf, axis=-1, keepdims=True) + EPS)
    return (y * w.astype(jnp.float32)).astype(x.dtype)


def _l2norm(x):
    return x * lax.rsqrt(jnp.sum(x * x, axis=-1, keepdims=True) + EPS)


def _dwconv(x, w):
    width = w.shape[0]
    pad = width // 2
    S = x.shape[1]
    xp = jnp.pad(x, ((0, 0), (pad, pad), (0, 0)))
    out = xp[:, 0:S, :] * w[0]
    for i in range(1, width):
        out = out + xp[:, i:i + S, :] * w[i]
    return out


def _rope_tables(S):
    inv = ROPE_THETA ** (-jnp.arange(0, ROT_DIM, 2, dtype=jnp.float32) / ROT_DIM)
    ang = jnp.arange(S, dtype=jnp.float32)[:, None] * inv[None, :]
    return jnp.cos(ang), jnp.sin(ang)


def _partial_rope(x, cos, sin):
    half = ROT_DIM // 2
    c = cos[None, :, None, None, :].astype(x.dtype)
    s = sin[None, :, None, None, :].astype(x.dtype)
    x1, x2, rest = x[..., :half], x[..., half:ROT_DIM], x[..., ROT_DIM:]
    return jnp.concatenate([x1 * c - x2 * s, x2 * c + x1 * s, rest], axis=-1)


def _gated_delta_chunked(q, k, v, g, beta):
    Bn, S, H, DK = q.shape
    DV = v.shape[-1]
    N = S // GDN_CHUNK

    def to_chunks(t):
        t = t.reshape((Bn, N, GDN_CHUNK, H) + t.shape[3:])
        return jnp.moveaxis(t, 3, 1)

    q, k, v, g, beta = (to_chunks(t) for t in (q, k, v, g, beta))
    g = jnp.cumsum(g, axis=-1)
    idx = jnp.arange(GDN_CHUNK)
    tri_incl = idx[:, None] >= idx[None, :]
    tri_strict = idx[:, None] > idx[None, :]
    decay = jnp.exp(jnp.where(tri_incl, g[..., :, None] - g[..., None, :], -jnp.inf))
    kb = k * beta[..., None]
    a_low = jnp.where(tri_strict, jnp.einsum('bhncd,bhnsd->bhncs', kb, k) * decay, 0.0)
    lhs = a_low + jnp.eye(GDN_CHUNK, dtype=q.dtype)
    u = lax.linalg.triangular_solve(lhs, v * beta[..., None], left_side=True, lower=True, unit_diagonal=True)
    w = lax.linalg.triangular_solve(lhs, kb * jnp.exp(g)[..., None], left_side=True, lower=True, unit_diagonal=True)
    attn_intra = jnp.einsum('bhncd,bhnsd->bhncs', q, k) * decay

    def step(state, inp):
        q_c, k_c, u_c, w_c, g_c, a_c = inp
        v_new = u_c - jnp.einsum('bhcd,bhde->bhce', w_c, state)
        o = (jnp.einsum('bhcd,bhde->bhce', q_c * jnp.exp(g_c)[..., None], state)
             + jnp.einsum('bhcs,bhse->bhce', a_c, v_new))
        g_last = g_c[..., -1]
        state = (state * jnp.exp(g_last)[..., None, None]
                 + jnp.einsum('bhcd,bhce->bhde', k_c * jnp.exp(g_last[..., None] - g_c)[..., None], v_new))
        return state, o

    xs = tuple(jnp.moveaxis(t, 2, 0) for t in (q, k, u, w, g, attn_intra))
    s0 = jnp.zeros((Bn, H, DK, DV), q.dtype)
    _, o = lax.scan(step, s0, xs)
    o = jnp.moveaxis(o, 0, 2)
    return jnp.moveaxis(o, 1, 3).reshape(Bn, S, H, DV)


def _gdn_branch(qkv, dec, bet, z, conv, A_log, dt_bias, norm_w):
    Bn, S, _ = qkv.shape
    f32 = jnp.float32
    qkv = jax.nn.silu(_dwconv(qkv, conv)).astype(f32)
    q, k, v = (t.reshape(Bn, S, H_GDN, DK_GDN) for t in jnp.split(qkv, 3, axis=-1))
    q = _l2norm(q) * DK_GDN ** -0.5
    k = _l2norm(k)
    beta = jax.nn.sigmoid(bet.astype(f32)).reshape(Bn, S, 2, H_GDN)
    g = -jnp.exp(A_log.astype(f32)) * jax.nn.softplus(dec.astype(f32).reshape(Bn, S, 2, H_GDN) + dt_bias.astype(f32))
    o_fwd = _gated_delta_chunked(q, k, v, g[:, :, 0], beta[:, :, 0])
    rev = lambda t: jnp.flip(t, axis=1)
    o_bwd = rev(_gated_delta_chunked(rev(q), rev(k), rev(v), rev(g[:, :, 1]), rev(beta[:, :, 1])))
    o = _rmsnorm(o_fwd + o_bwd, norm_w) * jax.nn.silu(z.astype(f32).reshape(Bn, S, H_GDN, DK_GDN))
    return o.reshape(Bn, S, W_BR).astype(z.dtype)


def _diff_attention(q, k, v, lam):
    Bn, S = q.shape[:2]
    nb = S // Q_BLOCK
    qb = jnp.moveaxis(q.reshape(Bn, nb, Q_BLOCK, H_DIFF, 2, DQK_DIFF), 1, 0)
    scale = DQK_DIFF ** -0.5

    def block(qi):
        s = jnp.einsum('bqhjd,bkhjd->bhjqk', qi, k).astype(jnp.float32) * scale
        p = jax.nn.softmax(s, axis=-1)
        p = (p[:, :, 0] - lam * p[:, :, 1]).astype(v.dtype)
        return jnp.einsum('bhqk,bkhe->bqhe', p, v)

    o = lax.map(block, qb)
    return jnp.moveaxis(o, 0, 1).reshape(Bn, S, H_DIFF, DV_DIFF)


def _diff_branch(q, k, v, z, lam_p, norm_w, lambda_init, cos, sin):
    Bn, S, _ = q.shape
    q = _partial_rope(q.reshape(Bn, S, H_DIFF, 2, DQK_DIFF), cos, sin)
    k = _partial_rope(k.reshape(Bn, S, H_DIFF, 2, DQK_DIFF), cos, sin)
    v = v.reshape(Bn, S, H_DIFF, DV_DIFF)
    lp = lam_p.astype(jnp.float32)
    lam = jnp.exp(jnp.sum(lp[0] * lp[1])) - jnp.exp(jnp.sum(lp[2] * lp[3])) + lambda_init
    o = _diff_attention(q, k, v, lam)
    o = _rmsnorm(o, norm_w) * (1.0 - lambda_init)
    return (o * jax.nn.silu(z).reshape(Bn, S, H_DIFF, DV_DIFF)).reshape(Bn, S, W_BR)


def _conv_branch(bg, cg, xin, z, w):
    return bg * _dwconv(cg * xin, w) * jax.nn.silu(z)


def _memory_branch(q, z, mem, norm_m, w_kv):
    Bn, S, _ = q.shape
    M = mem.shape[1]
    k, v = jnp.split(_rmsnorm(mem, norm_m) @ w_kv, 2, axis=-1)
    k = k.reshape(Bn, M, H_MEM, D_MEM_HEAD)
    v = v.reshape(Bn, M, H_MEM, D_MEM_HEAD)
    q = q.reshape(Bn, S, H_MEM, D_MEM_HEAD)
    s = jnp.einsum('bshd,bmhd->bhsm', q, k).astype(jnp.float32) * D_MEM_HEAD ** -0.5
    p = jax.nn.softmax(s, axis=-1).astype(v.dtype)
    o = jnp.einsum('bhsm,bmhd->bshd', p, v).reshape(Bn, S, W_BR)
    return o * jax.nn.silu(z)


def _trunk(x, mem, norm_pre, norm_post, norm_mem, w_in, gdn_conv, gdn_A_log, gdn_dt_bias, gdn_norm,
           diff_lambda, diff_norm, conv_w, w_mem_kv, w_out):
    cos, sin = _rope_tables(x.shape[1])
    for l in range(DEPTH):
        lambda_init = 0.8 - 0.6 * math.exp(-0.3 * l)
        h = _rmsnorm(x, norm_pre[l])
        (a_qkv, a_dec, a_beta, a_z, b_q, b_k, b_v, b_z,
         c_b, c_c, c_x, c_z, m_q, m_z) = jnp.split(h @ w_in[l], IN_SPLIT_IDX, axis=-1)
        y_a = _gdn_branch(a_qkv, a_dec, a_beta, a_z, gdn_conv[l], gdn_A_log[l], gdn_dt_bias[l], gdn_norm[l])
        y_b = _diff_branch(b_q, b_k, b_v, b_z, diff_lambda[l], diff_norm[l], lambda_init, cos, sin)
        y_c = _conv_branch(c_b, c_c, c_x, c_z, conv_w[l])
        y_m = _memory_branch(m_q, m_z, mem, norm_mem[l], w_mem_kv[l])
        y = jnp.concatenate([y_a, y_b, y_c, y_m], axis=-1) @ w_out[l]
        x = x + _rmsnorm(y, norm_post[l])
    return x


def setup_inputs(seed: int = 0) -> dict:
    key = jax.random.key(seed)
    ks = jax.random.split(key, 20)
    nrm = jax.random.normal
    dt = jnp.exp(jax.random.uniform(ks[10], (DEPTH, 2, H_GDN), minval=math.log(1e-3), maxval=math.log(1e-1)))
    return {
        'x_prompt': nrm(ks[0], (BATCH, SEQ, D_MODEL), jnp.float32),
        'x_sample': nrm(ks[1], (DEC_BATCH, DEC_SEQ, D_MODEL), jnp.float32),
        'mem_prompt': nrm(ks[2], (BATCH, N_MEM, D_MODEL), jnp.float32),
        'mem_sample': nrm(ks[3], (DEC_BATCH, N_MEM, D_MODEL), jnp.float32),
        'norm_pre': 1.0 + 0.01 * nrm(ks[4], (DEPTH, D_MODEL), jnp.float32),
        'norm_post': 1.0 + 0.01 * nrm(ks[5], (DEPTH, D_MODEL), jnp.float32),
        'norm_mem': 1.0 + 0.01 * nrm(ks[6], (DEPTH, D_MODEL), jnp.float32),
        'w_in': nrm(ks[7], (DEPTH, D_MODEL, IN_COLS), jnp.float32) * D_MODEL ** -0.5,
        'gdn_conv': nrm(ks[8], (DEPTH, GDN_CONV, 3 * W_BR), jnp.float32) * GDN_CONV ** -0.5,
        'gdn_A_log': jnp.log(jax.random.uniform(ks[9], (DEPTH, 2, H_GDN), minval=1.0, maxval=16.0)),
        'gdn_dt_bias': jnp.log(jnp.expm1(dt)),
        'gdn_norm': 1.0 + 0.01 * nrm(ks[11], (DEPTH, DK_GDN), jnp.float32),
        'diff_lambda': 0.1 * nrm(ks[12], (DEPTH, 4, DQK_DIFF), jnp.float32),
        'diff_norm': 1.0 + 0.01 * nrm(ks[13], (DEPTH, DV_DIFF), jnp.float32),
        'conv_w': nrm(ks[14], (DEPTH, CONV_W, W_BR), jnp.float32) * CONV_W ** -0.5,
        'w_mem_kv': nrm(ks[15], (DEPTH, D_MODEL, 2 * W_BR), jnp.float32) * D_MODEL ** -0.5,
        'w_out': nrm(ks[16], (DEPTH, D_MIX, D_MODEL), jnp.float32) * D_MIX ** -0.5,
    }


def reference(x_prompt, x_sample, mem_prompt, mem_sample, norm_pre, norm_post, norm_mem, w_in, gdn_conv,
              gdn_A_log, gdn_dt_bias, gdn_norm, diff_lambda, diff_norm, conv_w, w_mem_kv, w_out):
    y_prompt = _trunk(x_prompt, mem_prompt, norm_pre, norm_post, norm_mem, w_in, gdn_conv, gdn_A_log,
                      gdn_dt_bias, gdn_norm, diff_lambda, diff_norm, conv_w, w_mem_kv, w_out)
    y_sample = _trunk(x_sample, mem_sample, norm_pre, norm_post, norm_mem, w_in, gdn_conv, gdn_A_log,
                      gdn_dt_bias, gdn_norm, diff_lambda, diff_norm, conv_w, w_mem_kv, w_out)
    return (y_prompt, y_sample)
```

```python
import functools
import math

import jax
import jax.numpy as jnp
from jax import lax
from jax.experimental import pallas as pl
from jax.experimental.pallas import tpu as pltpu

F32 = jnp.float32
BF16 = jnp.bfloat16
EPS = 1e-6

W_BR = 512
N_HEADS = 4
D_HEAD = 128
DQK = 64
ROT_DIM = 16
ROPE_THETA = 500000.0
GDN_CONV = 5
GDN_CHUNK = 64
CONV_W = 3
N_MAIN = 14 * W_BR
LANES = 128
HALO = 16
NEG = -1e30
VMEM_LIMIT = 56 * 1024 * 1024

COL_A_Z = 3
COL_B_Q, COL_B_K, COL_B_V, COL_B_Z = 4, 5, 6, 7
COL_C_B, COL_C_C, COL_C_X, COL_C_Z = 8, 9, 10, 11
COL_M_Q, COL_M_Z = 12, 13


def _cparams(*sem):
    return pltpu.CompilerParams(dimension_semantics=sem, vmem_limit_bytes=VMEM_LIMIT)


def _sigmoid(x):
    return 1.0 / (1.0 + jnp.exp(-x))


def _silu(x):
    return x * _sigmoid(x)


def _softplus(x):
    return jnp.maximum(x, 0.0) + jnp.log(1.0 + jnp.exp(-jnp.abs(x)))


def _dot(a, b):
    return jnp.dot(a.astype(BF16), b.astype(BF16), preferred_element_type=F32)


def _dot_nt(a, b):
    return lax.dot_general(a.astype(BF16), b.astype(BF16), (((1,), (1,)), ((), ())),
                           preferred_element_type=F32)


def _dot_tn(a, b):
    return lax.dot_general(a.astype(BF16), b.astype(BF16), (((0,), (0,)), ((), ())),
                           preferred_element_type=F32)


def _same_block(r_i, c_i, size):
    shift = size.bit_length() - 1
    return jnp.right_shift(r_i, shift) == jnp.right_shift(c_i, shift)


def _split3(x):
    hi = x.astype(BF16)
    r1 = x - hi.astype(F32)
    mid = r1.astype(BF16)
    lo = (r1 - mid.astype(F32)).astype(BF16)
    return hi, mid, lo


def _in_proj_kernel(x_ref, nw_ref, w_ref, wgb_ref, alog_ref, dtb_ref, o_ref, gb_ref, h_ref):
    @pl.when(pl.program_id(2) == 0)
    def _():
        x = x_ref[...]
        ms = jnp.mean(x * x, axis=-1, keepdims=True)
        h = (x * lax.rsqrt(ms + EPS) * nw_ref[...]).astype(BF16)
        h_ref[...] = h
        raw = jnp.dot(h, wgb_ref[...], preferred_element_type=F32)
        lane = lax.broadcasted_iota(jnp.int32, raw.shape, 1)
        g = -jnp.exp(alog_ref[...]) * _softplus(raw + dtb_ref[...])
        is_beta = (lane >= 8) & (lane < 16)
        gb_ref[...] = jnp.where(is_beta, _sigmoid(raw), g)

    o_ref[...] = jnp.dot(h_ref[...], w_ref[...], preferred_element_type=F32).astype(o_ref.dtype)


def _in_proj(x, nw, w_main, w_gb, alog, dtb):
    B, S, D = x.shape
    tm = min(1024, S)
    tn = 1024
    return pl.pallas_call(
        _in_proj_kernel,
        out_shape=(jax.ShapeDtypeStruct((B, S, N_MAIN), BF16),
                   jax.ShapeDtypeStruct((B, S, LANES), F32)),
        grid=(B, S // tm, N_MAIN // tn),
        in_specs=[
            pl.BlockSpec((None, tm, D), lambda b, i, j: (b, i, 0)),
            pl.BlockSpec((1, D), lambda b, i, j: (0, 0)),
            pl.BlockSpec((D, tn), lambda b, i, j: (0, j)),
            pl.BlockSpec((D, LANES), lambda b, i, j: (0, 0)),
            pl.BlockSpec((1, LANES), lambda b, i, j: (0, 0)),
            pl.BlockSpec((1, LANES), lambda b, i, j: (0, 0)),
        ],
        out_specs=(pl.BlockSpec((None, tm, tn), lambda b, i, j: (b, i, j)),
                   pl.BlockSpec((None, tm, LANES), lambda b, i, j: (b, i, 0))),
        scratch_shapes=[pltpu.VMEM((tm, D), BF16)],
        compiler_params=_cparams("parallel", "parallel", "arbitrary"),
        name="in_proj",
    )(x, nw, w_main, w_gb, alog, dtb)


def _kv_proj_kernel(x_ref, nw_ref, w_ref, o_ref):
    x = x_ref[...]
    ms = jnp.mean(x * x, axis=-1, keepdims=True)
    h = (x * lax.rsqrt(ms + EPS) * nw_ref[...]).astype(BF16)
    o_ref[...] = jnp.dot(h, w_ref[...], preferred_element_type=F32).astype(o_ref.dtype)


def _kv_proj(mem, nw, w_kv):
    B, M, D = mem.shape
    n_out = w_kv.shape[1]
    return pl.pallas_call(
        _kv_proj_kernel,
        out_shape=jax.ShapeDtypeStruct((B, M, n_out), BF16),
        grid=(B,),
        in_specs=[
            pl.BlockSpec((None, M, D), lambda b: (b, 0, 0)),
            pl.BlockSpec((1, D), lambda b: (0, 0)),
            pl.BlockSpec((D, n_out), lambda b: (0, 0)),
        ],
        out_specs=pl.BlockSpec((None, M, n_out), lambda b: (b, 0, 0)),
        compiler_params=_cparams("parallel"),
        name="kv_proj",
    )(mem, nw, w_kv)


def _gdn_prep_kernel(xp_ref, x_ref, xn_ref, gb_ref, cw_ref, qkv_ref, col_ref):
    i = pl.program_id(1)
    n_i = pl.num_programs(1)
    T = x_ref.shape[0]
    pad = GDN_CONV // 2
    prev = jnp.where(i > 0, xp_ref[HALO - 8:, :].astype(F32), 0.0)
    nxt = jnp.where(i < n_i - 1, xn_ref[:8, :].astype(F32), 0.0)
    xe = jnp.concatenate([prev, x_ref[...].astype(F32), nxt], axis=0)
    acc = xe[8 - pad:8 - pad + T, :] * cw_ref[0:1, :]
    for d in range(1, GDN_CONV):
        acc = acc + xe[8 - pad + d:8 - pad + d + T, :] * cw_ref[d:d + 1, :]
    y = _silu(acc)
    for h in range(2 * N_HEADS):
        sl = slice(h * D_HEAD, (h + 1) * D_HEAD)
        t = y[:, sl]
        r = lax.rsqrt(jnp.sum(t * t, axis=-1, keepdims=True) + EPS)
        if h < N_HEADS:
            r = r * (D_HEAD ** -0.5)
        qkv_ref[:, sl] = t * r
    qkv_ref[:, 2 * W_BR:] = y[:, 2 * W_BR:]

    gb = gb_ref[...]
    r_i = lax.broadcasted_iota(jnp.int32, (T, T), 0)
    c_i = lax.broadcasted_iota(jnp.int32, (T, T), 1)
    same = _same_block(r_i, c_i, GDN_CHUNK)
    m_f = jnp.where(same & (c_i <= r_i), 1.0, 0.0).astype(BF16)
    m_b = jnp.where(same & (c_i >= r_i), 1.0, 0.0).astype(BF16)
    m_t = jnp.where(same, 1.0, 0.0).astype(BF16)
    parts = _split3(gb)
    cum_f = sum(jnp.dot(m_f, p, preferred_element_type=F32) for p in parts)
    cum_b = sum(jnp.dot(m_b, p, preferred_element_type=F32) for p in parts)
    tot = sum(jnp.dot(m_t, p, preferred_element_type=F32) for p in parts)
    lane = lax.broadcasted_iota(jnp.int32, gb.shape, 1)
    col_ref[...] = jnp.where(lane < 4, cum_f,
                             jnp.where(lane < 8, cum_b,
                                       jnp.where(lane < 16, gb, tot)))


def _gdn_prep(proj, gb, conv_w):
    B, S, _ = proj.shape
    T = min(512, S)
    nb = T // HALO
    n_halo = S // HALO
    wq = 3 * W_BR
    return pl.pallas_call(
        _gdn_prep_kernel,
        out_shape=(jax.ShapeDtypeStruct((B, S, wq), F32),
                   jax.ShapeDtypeStruct((B, S, LANES), F32)),
        grid=(B, S // T),
        in_specs=[
            pl.BlockSpec((None, HALO, wq), lambda b, i: (b, jnp.maximum(i * nb - 1, 0), 0)),
            pl.BlockSpec((None, T, wq), lambda b, i: (b, i, 0)),
            pl.BlockSpec((None, HALO, wq), lambda b, i: (b, jnp.minimum((i + 1) * nb, n_halo - 1), 0)),
            pl.BlockSpec((None, T, LANES), lambda b, i: (b, i, 0)),
            pl.BlockSpec((GDN_CONV, wq), lambda b, i: (0, 0)),
        ],
        out_specs=(pl.BlockSpec((None, T, wq), lambda b, i: (b, i, 0)),
                   pl.BlockSpec((None, T, LANES), lambda b, i: (b, i, 0))),
        compiler_params=_cparams("parallel", "parallel"),
        name="gdn_prep",
    )(proj, proj, proj, gb, conv_w)


def _gdn_masks(reverse):
    n = N_HEADS * GDN_CHUNK
    r_i = lax.broadcasted_iota(jnp.int32, (n, n), 0)
    c_i = lax.broadcasted_iota(jnp.int32, (n, n), 1)
    same_head = _same_block(r_i, c_i, GDN_CHUNK)
    same_blk = _same_block(r_i, c_i, 8)
    if reverse:
        incl = same_head & (c_i >= r_i)
        strict = same_head & (c_i > r_i)
    else:
        incl = same_head & (c_i <= r_i)
        strict = same_head & (c_i < r_i)
    return incl, strict, same_blk


def _stack_heads(x):
    return jnp.concatenate([x[:, h * D_HEAD:(h + 1) * D_HEAD] for h in range(N_HEADS)], axis=0)


def _unit_tri_solve(a_mat, rhs, same_blk):
    n = a_mat.shape[0]
    d1 = jnp.where(same_blk, a_mat, 0.0)
    lo = a_mat - d1
    d2 = _dot(d1, d1)
    d4 = _dot(d2, d2)
    eye = jnp.where(lax.broadcasted_iota(jnp.int32, (n, n), 0)
                    == lax.broadcasted_iota(jnp.int32, (n, n), 1), 1.0, 0.0)
    p1 = eye - d1 + d2 - _dot(d1, d2)
    t8 = p1 + _dot(p1, d4)
    by = _dot(t8, jnp.concatenate([lo, rhs], axis=1))
    b1 = by[:, :n]
    y = by[:, n:]
    b2 = _dot(b1, b1)
    b4 = _dot(b2, b2)
    z = y - _dot(b1, y)
    z = z + _dot(b2, z)
    return z + _dot(b4, z)


def _gdn_chunk(q, k, v, col, row, s_ref, d, masks):
    incl, strict, same_blk = masks
    g_idx = d * N_HEADS
    b_idx = 8 + d * N_HEADS
    t_idx = 16 + d * N_HEADS
    qs, kbs, ks, rhs_v, rhs_k, gcs, kds = [], [], [], [], [], [], []
    for h in range(N_HEADS):
        sl = slice(h * D_HEAD, (h + 1) * D_HEAD)
        gc = col[:, g_idx + h:g_idx + h + 1]
        bc = col[:, b_idx + h:b_idx + h + 1]
        gt = col[:, t_idx + h:t_idx + h + 1]
        eg = jnp.exp(gc)
        k_h = k[:, sl]
        kb = k_h * bc
        qs.append(q[:, sl] * eg)
        kbs.append(kb)
        ks.append(k_h)
        rhs_v.append(v[:, sl] * bc)
        rhs_k.append(kb * eg)
        gcs.append(jnp.broadcast_to(gc, (GDN_CHUNK, N_HEADS * GDN_CHUNK)))
        kds.append(k_h * jnp.exp(gt - gc))
    k_st = jnp.concatenate(ks, axis=0)
    gram = _dot_nt(jnp.concatenate(kbs + [_stack_heads(q)], axis=0), k_st)
    n = N_HEADS * GDN_CHUNK
    dlog = jnp.concatenate(gcs, axis=0) - row[d:d + 1, :]
    decay = jnp.exp(jnp.where(incl, dlog, NEG))
    a_mat = jnp.where(strict, gram[:n] * decay, 0.0)
    attn = gram[n:] * decay
    rhs = jnp.concatenate([jnp.concatenate(rhs_v, axis=0), jnp.concatenate(rhs_k, axis=0)], axis=1)
    x = _unit_tri_solve(a_mat, rhs, same_blk)
    v_news, q_states = [], []
    for h in range(N_HEADS):
        rs = slice(h * GDN_CHUNK, (h + 1) * GDN_CHUNK)
        st = s_ref[d, h]
        ws = _dot(jnp.concatenate([x[rs, D_HEAD:], qs[h]], axis=0), st)
        v_news.append(x[rs, :D_HEAD] - ws[:GDN_CHUNK])
        q_states.append(ws[GDN_CHUNK:])
    v_new = jnp.concatenate(v_news, axis=0)
    av = _dot(attn, v_new)
    outs = []
    for h in range(N_HEADS):
        rs = slice(h * GDN_CHUNK, (h + 1) * GDN_CHUNK)
        outs.append(q_states[h] + av[rs])
        e_tot = jnp.exp(row[8 + g_idx + h:8 + g_idx + h + 1, :D_HEAD])
        s_ref[d, h] = s_ref[d, h] * e_tot + _dot_tn(kds[h], v_news[h])
    return jnp.concatenate(outs, axis=1)


def _gdn_kernel(qf_ref, kf_ref, vf_ref, colf_ref, rowf_ref,
                qb_ref, kb_ref, vb_ref, colb_ref, rowb_ref,
                of_ref, ob_ref, s_ref):
    nc = qf_ref.shape[0]

    @pl.when(pl.program_id(1) == 0)
    def _():
        s_ref[...] = jnp.zeros(s_ref.shape, F32)

    masks_f = _gdn_masks(False)
    masks_b = _gdn_masks(True)

    def body(c, carry):
        cb = nc - 1 - c
        of_ref[c] = _gdn_chunk(qf_ref[c], kf_ref[c], vf_ref[c], colf_ref[c], rowf_ref[c],
                               s_ref, 0, masks_f)
        ob_ref[cb] = _gdn_chunk(qb_ref[cb], kb_ref[cb], vb_ref[cb], colb_ref[cb], rowb_ref[cb],
                                s_ref, 1, masks_b)
        return carry

    lax.fori_loop(0, nc, body, 0)


def _gdn(qkv, col, row):
    B, N, C, _ = qkv.shape
    nc = min(4, N)
    nblk = N // nc
    fwd = lambda j: (lambda b, i: (b, i, 0, j))
    bwd = lambda j: (lambda b, i: (b, nblk - 1 - i, 0, j))
    specs = []
    for mk in (fwd, bwd):
        specs += [pl.BlockSpec((None, nc, C, W_BR), mk(0)),
                  pl.BlockSpec((None, nc, C, W_BR), mk(1)),
                  pl.BlockSpec((None, nc, C, W_BR), mk(2)),
                  pl.BlockSpec((None, nc, C, LANES), mk(0)),
                  pl.BlockSpec((None, nc, 16, 2 * LANES), mk(0))]
    out_sds = jax.ShapeDtypeStruct((B, N, C, W_BR), F32)
    return pl.pallas_call(
        _gdn_kernel,
        out_shape=(out_sds, out_sds),
        grid=(B, nblk),
        in_specs=specs,
        out_specs=(pl.BlockSpec((None, nc, C, W_BR), fwd(0)),
                   pl.BlockSpec((None, nc, C, W_BR), bwd(0))),
        scratch_shapes=[pltpu.VMEM((2, N_HEADS, D_HEAD, D_HEAD), F32)],
        compiler_params=_cparams("parallel", "arbitrary"),
        name="gdn",
    )(qkv, qkv, qkv, col, row, qkv, qkv, qkv, col, row)


def _gdn_post_kernel(of_ref, ob_ref, z_ref, nw_ref, y_ref):
    o = of_ref[...] + ob_ref[...]
    z = z_ref[...].astype(F32)
    for h in range(N_HEADS):
        sl = slice(h * D_HEAD, (h + 1) * D_HEAD)
        t = o[:, sl]
        r = lax.rsqrt(jnp.mean(t * t, axis=-1, keepdims=True) + EPS)
        y_ref[:, sl] = (t * r * nw_ref[...] * _silu(z[:, sl])).astype(y_ref.dtype)


def _gdn_post(o_f, o_b, proj, nw):
    B, S, _ = o_f.shape
    T = min(1024, S)
    return pl.pallas_call(
        _gdn_post_kernel,
        out_shape=jax.ShapeDtypeStruct((B, S, W_BR), BF16),
        grid=(B, S // T),
        in_specs=[
            pl.BlockSpec((None, T, W_BR), lambda b, i: (b, i, 0)),
            pl.BlockSpec((None, T, W_BR), lambda b, i: (b, i, 0)),
            pl.BlockSpec((None, T, W_BR), lambda b, i: (b, i, COL_A_Z)),
            pl.BlockSpec((1, D_HEAD), lambda b, i: (0, 0)),
        ],
        out_specs=pl.BlockSpec((None, T, W_BR), lambda b, i: (b, i, 0)),
        compiler_params=_cparams("parallel", "parallel"),
        name="gdn_post",
    )(o_f, o_b, proj, nw)


def _rope_kernel(q_ref, k_ref, c_ref, sa_ref, sb_ref, qo_ref, ko_ref):
    c = c_ref[...]
    sa = sa_ref[...]
    sb = sb_ref[...]
    half = ROT_DIM // 2
    for src, dst, scale in ((q_ref, qo_ref, DQK ** -0.5), (k_ref, ko_ref, 1.0)):
        for h in range(N_HEADS):
            sl = slice(h * D_HEAD, (h + 1) * D_HEAD)
            x = src[:, sl].astype(F32)
            y = (x * c + pltpu.roll(x, half, 1) * sa + pltpu.roll(x, D_HEAD - half, 1) * sb)
            dst[:, sl] = (y * scale).astype(dst.dtype)


def _rope(proj, cos_t, sin_a, sin_b):
    B, S, _ = proj.shape
    T = min(1024, S)
    sds = jax.ShapeDtypeStruct((B, S, W_BR), BF16)
    tab = pl.BlockSpec((T, D_HEAD), lambda b, i: (i, 0))
    return pl.pallas_call(
        _rope_kernel,
        out_shape=(sds, sds),
        grid=(B, S // T),
        in_specs=[
            pl.BlockSpec((None, T, W_BR), lambda b, i: (b, i, COL_B_Q)),
            pl.BlockSpec((None, T, W_BR), lambda b, i: (b, i, COL_B_K)),
            tab, tab, tab,
        ],
        out_specs=(pl.BlockSpec((None, T, W_BR), lambda b, i: (b, i, 0)),
                   pl.BlockSpec((None, T, W_BR), lambda b, i: (b, i, 0))),
        compiler_params=_cparams("parallel", "parallel"),
        name="rope",
    )(proj, proj, cos_t, sin_a, sin_b)


def _diff_attn_kernel(q_ref, k_ref, v_ref, z_ref, lp_ref, nw_ref, y_ref, *, lambda_init):
    q = q_ref[...]
    k = k_ref[...]
    lane = lax.broadcasted_iota(jnp.int32, q.shape, 1)
    zero = jnp.zeros_like(q)
    lp = lp_ref[...]
    lam = (jnp.exp(jnp.sum(lp[0:1] * lp[1:2], axis=-1, keepdims=True))
           - jnp.exp(jnp.sum(lp[2:3] * lp[3:4], axis=-1, keepdims=True)) + lambda_init)
    probs = []
    for first in (True, False):
        qm = jnp.where((lane < DQK) == first, q, zero)
        s = _dot_nt(qm, k)
        p = jnp.exp(s - jnp.max(s, axis=-1, keepdims=True))
        inv = 1.0 / jnp.sum(p, axis=-1, keepdims=True)
        probs.append(p * (inv if first else inv * lam))
    p = (probs[0] - probs[1]).astype(BF16)
    o = jnp.dot(p, v_ref[...], preferred_element_type=F32)
    r = lax.rsqrt(jnp.mean(o * o, axis=-1, keepdims=True) + EPS)
    o = o * r * nw_ref[...] * (1.0 - lambda_init)
    y_ref[...] = (o * _silu(z_ref[...].astype(F32))).astype(y_ref.dtype)


def _diff_attn(q_rot, k_rot, proj, lam_p, nw, lambda_init):
    B, S, _ = q_rot.shape
    tq = min(256, S)
    nh = W_BR // D_HEAD
    return pl.pallas_call(
        functools.partial(_diff_attn_kernel, lambda_init=lambda_init),
        out_shape=jax.ShapeDtypeStruct((B, S, W_BR), BF16),
        grid=(B, N_HEADS, S // tq),
        in_specs=[
            pl.BlockSpec((None, tq, D_HEAD), lambda b, h, i: (b, i, h)),
            pl.BlockSpec((None, S, D_HEAD), lambda b, h, i: (b, 0, h)),
            pl.BlockSpec((None, S, D_HEAD), lambda b, h, i: (b, 0, COL_B_V * nh + h)),
            pl.BlockSpec((None, tq, D_HEAD), lambda b, h, i: (b, i, COL_B_Z * nh + h)),
            pl.BlockSpec((4, DQK), lambda b, h, i: (0, 0)),
            pl.BlockSpec((1, D_HEAD), lambda b, h, i: (0, 0)),
        ],
        out_specs=pl.BlockSpec((None, tq, D_HEAD), lambda b, h, i: (b, i, h)),
        compiler_params=_cparams("parallel", "parallel", "parallel"),
        name="diff_attn",
    )(q_rot, k_rot, proj, proj, lam_p, nw)


def _conv_mem_kernel(cb_ref, cc_ref, cx_ref, cz_ref, ccp_ref, cxp_ref, ccn_ref, cxn_ref,
                     mq_ref, mz_ref, kv_ref, cw_ref, y_ref):
    i = pl.program_id(1)
    n_i = pl.num_programs(1)
    T = cb_ref.shape[0]
    prev = jnp.where(i > 0, ccp_ref[HALO - 8:, :].astype(F32) * cxp_ref[HALO - 8:, :].astype(F32), 0.0)
    nxt = jnp.where(i < n_i - 1, ccn_ref[:8, :].astype(F32) * cxn_ref[:8, :].astype(F32), 0.0)
    cur = cc_ref[...].astype(F32) * cx_ref[...].astype(F32)
    xe = jnp.concatenate([prev, cur, nxt], axis=0)
    pad = CONV_W // 2
    acc = xe[8 - pad:8 - pad + T, :] * cw_ref[0:1, :]
    for d in range(1, CONV_W):
        acc = acc + xe[8 - pad + d:8 - pad + d + T, :] * cw_ref[d:d + 1, :]
    y_c = cb_ref[...].astype(F32) * acc * _silu(cz_ref[...].astype(F32))
    y_ref[:, :W_BR] = y_c.astype(y_ref.dtype)

    mz = mz_ref[...].astype(F32)
    for h in range(N_HEADS):
        sl = slice(h * D_HEAD, (h + 1) * D_HEAD)
        s = _dot_nt(mq_ref[:, sl], kv_ref[:, sl]) * (D_HEAD ** -0.5)
        p = jnp.exp(s - jnp.max(s, axis=-1, keepdims=True))
        p = (p * (1.0 / jnp.sum(p, axis=-1, keepdims=True))).astype(BF16)
        o = jnp.dot(p, kv_ref[:, W_BR + h * D_HEAD:W_BR + (h + 1) * D_HEAD],
                    preferred_element_type=F32)
        y_ref[:, W_BR + h * D_HEAD:W_BR + (h + 1) * D_HEAD] = (o * _silu(mz[:, sl])).astype(y_ref.dtype)


def _conv_mem(proj, kv, conv_w):
    B, S, _ = proj.shape
    M = kv.shape[1]
    T = min(512, S)
    nb = T // HALO
    n_halo = S // HALO
    main = lambda c: pl.BlockSpec((None, T, W_BR), lambda b, i: (b, i, c))
    prev = lambda c: pl.BlockSpec((None, HALO, W_BR), lambda b, i: (b, jnp.maximum(i * nb - 1, 0), c))
    nxt = lambda c: pl.BlockSpec((None, HALO, W_BR),
                                 lambda b, i: (b, jnp.minimum((i + 1) * nb, n_halo - 1), c))
    return pl.pallas_call(
        _conv_mem_kernel,
        out_shape=jax.ShapeDtypeStruct((B, S, 2 * W_BR), BF16),
        grid=(B, S // T),
        in_specs=[main(COL_C_B), main(COL_C_C), main(COL_C_X), main(COL_C_Z),
                  prev(COL_C_C), prev(COL_C_X), nxt(COL_C_C), nxt(COL_C_X),
                  main(COL_M_Q), main(COL_M_Z),
                  pl.BlockSpec((None, M, 2 * W_BR), lambda b, i: (b, 0, 0)),
                  pl.BlockSpec((CONV_W, W_BR), lambda b, i: (0, 0))],
        out_specs=pl.BlockSpec((None, T, 2 * W_BR), lambda b, i: (b, i, 0)),
        compiler_params=_cparams("parallel", "parallel"),
        name="conv_mem",
    )(proj, proj, proj, proj, proj, proj, proj, proj, proj, proj, kv, conv_w)


def _out_proj_kernel(ya_ref, yb_ref, ycm_ref, w_ref, x_ref, nw_ref, o_ref):
    y = jnp.dot(ya_ref[...], w_ref[0:W_BR, :], preferred_element_type=F32)
    y = y + jnp.dot(yb_ref[...], w_ref[W_BR:2 * W_BR, :], preferred_element_type=F32)
    y = y + jnp.dot(ycm_ref[...], w_ref[2 * W_BR:, :], preferred_element_type=F32)
    r = lax.rsqrt(jnp.mean(y * y, axis=-1, keepdims=True) + EPS)
    o_ref[...] = x_ref[...] + y * r * nw_ref[...]


def _out_proj(y_a, y_b, y_cm, w_out, x, nw):
    B, S, D = x.shape
    T = min(512, S)
    return pl.pallas_call(
        _out_proj_kernel,
        out_shape=jax.ShapeDtypeStruct((B, S, D), F32),
        grid=(B, S // T),
        in_specs=[
            pl.BlockSpec((None, T, W_BR), lambda b, i: (b, i, 0)),
            pl.BlockSpec((None, T, W_BR), lambda b, i: (b, i, 0)),
            pl.BlockSpec((None, T, 2 * W_BR), lambda b, i: (b, i, 0)),
            pl.BlockSpec((4 * W_BR, D), lambda b, i: (0, 0)),
            pl.BlockSpec((None, T, D), lambda b, i: (b, i, 0)),
            pl.BlockSpec((1, D), lambda b, i: (0, 0)),
        ],
        out_specs=pl.BlockSpec((None, T, D), lambda b, i: (b, i, 0)),
        compiler_params=_cparams("parallel", "parallel"),
        name="out_proj",
    )(y_a, y_b, y_cm, w_out, x, nw)


def _rope_tables(S):
    half = ROT_DIM // 2
    inv = ROPE_THETA ** (-jnp.arange(0, ROT_DIM, 2, dtype=F32) / ROT_DIM)
    ang = jnp.arange(S, dtype=F32)[:, None] * inv[None, :]
    cos, sin = jnp.cos(ang), jnp.sin(ang)
    ones = jnp.ones((S, DQK - ROT_DIM), F32)
    zeros = jnp.zeros((S, DQK - ROT_DIM), F32)
    z8 = jnp.zeros((S, half), F32)
    c_map = jnp.concatenate([cos, cos, ones], axis=1)
    sa_map = jnp.concatenate([z8, sin, zeros], axis=1)
    sb_map = jnp.concatenate([-sin, z8, zeros], axis=1)
    two = lambda t: jnp.concatenate([t, t], axis=1)
    return two(c_map), two(sa_map), two(sb_map)


def _row_info(col, N):
    B = col.shape[0]
    c4 = col.reshape(B, N, GDN_CHUNK, LANES)
    g_st = jnp.swapaxes(c4[..., 0:8], 2, 3).reshape(B, N, 2, N_HEADS * GDN_CHUNK)
    tot = jnp.broadcast_to(c4[:, :, 0, 16:24][..., None], (B, N, 8, N_HEADS * GDN_CHUNK))
    pad = jnp.zeros((B, N, 6, N_HEADS * GDN_CHUNK), F32)
    return jnp.concatenate([g_st, pad, tot], axis=2)


def _prep_layer_weights(l, norm_pre, norm_post, norm_mem, w_in, gdn_conv, gdn_A_log, gdn_dt_bias,
                        gdn_norm, diff_lambda, diff_norm, conv_w, w_mem_kv, w_out):
    w = w_in[l]
    q_end = 3 * W_BR
    n_db = 2 * N_HEADS
    dec = w[:, q_end:q_end + n_db]
    bet = w[:, q_end + n_db:q_end + 2 * n_db]
    w_main = jnp.concatenate([w[:, :q_end], w[:, q_end + 2 * n_db:]], axis=1).astype(BF16)
    w_gb = jnp.concatenate([dec, bet, dec, jnp.zeros((w.shape[0], LANES - 3 * n_db), F32)],
                           axis=1).astype(BF16)
    lane_pad = lambda t: jnp.concatenate(
        [t.reshape(1, n_db), jnp.zeros((1, n_db), F32), t.reshape(1, n_db),
         jnp.zeros((1, LANES - 3 * n_db), F32)], axis=1)
    return dict(
        norm_pre=norm_pre[l][None, :], norm_post=norm_post[l][None, :], norm_mem=norm_mem[l][None, :],
        w_main=w_main, w_gb=w_gb, alog=lane_pad(gdn_A_log[l]), dtb=lane_pad(gdn_dt_bias[l]),
        gdn_conv=gdn_conv[l], gdn_norm=gdn_norm[l][None, :], diff_lambda=diff_lambda[l],
        diff_norm=diff_norm[l][None, :], conv_w=conv_w[l], w_kv=w_mem_kv[l].astype(BF16),
        w_out=w_out[l].astype(BF16))


def _layer(x, mem, p, lambda_init, tables):
    B, S, _ = x.shape
    N = S // GDN_CHUNK
    proj, gb = _in_proj(x, p["norm_pre"], p["w_main"], p["w_gb"], p["alog"], p["dtb"])
    kv = _kv_proj(mem, p["norm_mem"], p["w_kv"])
    qkv, col = _gdn_prep(proj, gb, p["gdn_conv"])
    row = _row_info(col, N)
    o_f, o_b = _gdn(qkv.reshape(B, N, GDN_CHUNK, 3 * W_BR), col.reshape(B, N, GDN_CHUNK, LANES), row)
    y_a = _gdn_post(o_f.reshape(B, S, W_BR), o_b.reshape(B, S, W_BR), proj, p["gdn_norm"])
    q_rot, k_rot = _rope(proj, *tables)
    y_b = _diff_attn(q_rot, k_rot, proj, p["diff_lambda"], p["diff_norm"], lambda_init)
    y_cm = _conv_mem(proj, kv, p["conv_w"])
    return _out_proj(y_a, y_b, y_cm, p["w_out"], x, p["norm_post"])


def _trunk(x, mem, layer_params):
    tables = _rope_tables(x.shape[1])
    for l, p in enumerate(layer_params):
        lambda_init = 0.8 - 0.6 * math.exp(-0.3 * l)
        x = _layer(x, mem, p, lambda_init, tables)
    return x


def kernel(x_prompt, x_sample, mem_prompt, mem_sample, norm_pre, norm_post, norm_mem, w_in, gdn_conv,
           gdn_A_log, gdn_dt_bias, gdn_norm, diff_lambda, diff_norm, conv_w, w_mem_kv, w_out):
    depth = w_in.shape[0]
    params = [_prep_layer_weights(l, norm_pre, norm_post, norm_mem, w_in, gdn_conv, gdn_A_log,
                                  gdn_dt_bias, gdn_norm, diff_lambda, diff_norm, conv_w, w_mem_kv, w_out)
              for l in range(depth)]
    return (_trunk(x_prompt, mem_prompt, params), _trunk(x_sample, mem_sample, params))
```

```python
import functools
import math

import jax
import jax.numpy as jnp
from jax import lax
from jax.experimental import pallas as pl
from jax.experimental.pallas import tpu as pltpu

F32 = jnp.float32
BF16 = jnp.bfloat16
EPS = 1e-6

W_BR = 512
N_HEADS = 4
D_HEAD = 128
DQK = 64
ROT_DIM = 16
ROPE_THETA = 500000.0
GDN_CONV = 5
GDN_CHUNK = 64
CONV_W = 3
N_MAIN = 14 * W_BR
LANES = 128
HALO = 16
NEG = -1e30
VMEM_LIMIT = 56 * 1024 * 1024

COL_A_Z = 3
COL_B_Q, COL_B_K, COL_B_V, COL_B_Z = 4, 5, 6, 7
COL_C_B, COL_C_C, COL_C_X, COL_C_Z = 8, 9, 10, 11
COL_M_Q, COL_M_Z = 12, 13


def _cparams(*sem):
    return pltpu.CompilerParams(dimension_semantics=sem, vmem_limit_bytes=VMEM_LIMIT)


def _sigmoid(x):
    return 1.0 / (1.0 + jnp.exp(-x))


def _silu(x):
    return x * _sigmoid(x)


def _softplus(x):
    return jnp.maximum(x, 0.0) + jnp.log(1.0 + jnp.exp(-jnp.abs(x)))


def _dot(a, b):
    return jnp.dot(a.astype(BF16), b.astype(BF16), preferred_element_type=F32)


def _dot_nt(a, b):
    return lax.dot_general(a.astype(BF16), b.astype(BF16), (((1,), (1,)), ((), ())),
                           preferred_element_type=F32)


def _dot_tn(a, b):
    return lax.dot_general(a.astype(BF16), b.astype(BF16), (((0,), (0,)), ((), ())),
                           preferred_element_type=F32)


def _same_block(r_i, c_i, size):
    shift = size.bit_length() - 1
    return jnp.right_shift(r_i, shift) == jnp.right_shift(c_i, shift)


def _split3(x):
    hi = x.astype(BF16)
    r1 = x - hi.astype(F32)
    mid = r1.astype(BF16)
    lo = (r1 - mid.astype(F32)).astype(BF16)
    return hi, mid, lo


def _in_proj_kernel(x_ref, nw_ref, w_ref, wgb_ref, alog_ref, dtb_ref, o_ref, gb_ref, h_ref):
    @pl.when(pl.program_id(2) == 0)
    def _():
        x = x_ref[...]
        ms = jnp.mean(x * x, axis=-1, keepdims=True)
        h = (x * lax.rsqrt(ms + EPS) * nw_ref[...]).astype(BF16)
        h_ref[...] = h
        raw = jnp.dot(h, wgb_ref[...], preferred_element_type=F32)
        lane = lax.broadcasted_iota(jnp.int32, raw.shape, 1)
        g = -jnp.exp(alog_ref[...]) * _softplus(raw + dtb_ref[...])
        is_beta = (lane >= 8) & (lane < 16)
        gb_ref[...] = jnp.where(is_beta, _sigmoid(raw), g)

    o_ref[...] = jnp.dot(h_ref[...], w_ref[...], preferred_element_type=F32).astype(o_ref.dtype)


def _in_proj(x, nw, w_main, w_gb, alog, dtb):
    B, S, D = x.shape
    tm = min(1024, S)
    tn = 1024
    return pl.pallas_call(
        _in_proj_kernel,
        out_shape=(jax.ShapeDtypeStruct((B, S, N_MAIN), BF16),
                   jax.ShapeDtypeStruct((B, S, LANES), F32)),
        grid=(B, S // tm, N_MAIN // tn),
        in_specs=[
            pl.BlockSpec((None, tm, D), lambda b, i, j: (b, i, 0)),
            pl.BlockSpec((1, D), lambda b, i, j: (0, 0)),
            pl.BlockSpec((D, tn), lambda b, i, j: (0, j)),
            pl.BlockSpec((D, LANES), lambda b, i, j: (0, 0)),
            pl.BlockSpec((1, LANES), lambda b, i, j: (0, 0)),
            pl.BlockSpec((1, LANES), lambda b, i, j: (0, 0)),
        ],
        out_specs=(pl.BlockSpec((None, tm, tn), lambda b, i, j: (b, i, j)),
                   pl.BlockSpec((None, tm, LANES), lambda b, i, j: (b, i, 0))),
        scratch_shapes=[pltpu.VMEM((tm, D), BF16)],
        compiler_params=_cparams("parallel", "parallel", "arbitrary"),
        name="in_proj",
    )(x, nw, w_main, w_gb, alog, dtb)


def _kv_proj_kernel(x_ref, nw_ref, w_ref, o_ref):
    x = x_ref[...]
    ms = jnp.mean(x * x, axis=-1, keepdims=True)
    h = (x * lax.rsqrt(ms + EPS) * nw_ref[...]).astype(BF16)
    o_ref[...] = jnp.dot(h, w_ref[...], preferred_element_type=F32).astype(o_ref.dtype)


def _kv_proj(mem, nw, w_kv):
    B, M, D = mem.shape
    n_out = w_kv.shape[1]
    return pl.pallas_call(
        _kv_proj_kernel,
        out_shape=jax.ShapeDtypeStruct((B, M, n_out), BF16),
        grid=(B,),
        in_specs=[
            pl.BlockSpec((None, M, D), lambda b: (b, 0, 0)),
            pl.BlockSpec((1, D), lambda b: (0, 0)),
            pl.BlockSpec((D, n_out), lambda b: (0, 0)),
        ],
        out_specs=pl.BlockSpec((None, M, n_out), lambda b: (b, 0, 0)),
        compiler_params=_cparams("parallel"),
        name="kv_proj",
    )(mem, nw, w_kv)


def _gdn_prep_kernel(xp_ref, x_ref, xn_ref, gb_ref, cw_ref, qkv_ref, col_ref):
    i = pl.program_id(1)
    n_i = pl.num_programs(1)
    T = x_ref.shape[0]
    pad = GDN_CONV // 2
    prev = jnp.where(i > 0, xp_ref[HALO - 8:, :].astype(F32), 0.0)
    nxt = jnp.where(i < n_i - 1, xn_ref[:8, :].astype(F32), 0.0)
    xe = jnp.concatenate([prev, x_ref[...].astype(F32), nxt], axis=0)
    acc = xe[8 - pad:8 - pad + T, :] * cw_ref[0:1, :]
    for d in range(1, GDN_CONV):
        acc = acc + xe[8 - pad + d:8 - pad + d + T, :] * cw_ref[d:d + 1, :]
    y = _silu(acc)
    for h in range(2 * N_HEADS):
        sl = slice(h * D_HEAD, (h + 1) * D_HEAD)
        t = y[:, sl]
        r = lax.rsqrt(jnp.sum(t * t, axis=-1, keepdims=True) + EPS)
        if h < N_HEADS:
            r = r * (D_HEAD ** -0.5)
        qkv_ref[:, sl] = t * r
    qkv_ref[:, 2 * W_BR:] = y[:, 2 * W_BR:]

    gb = gb_ref[...]
    r_i = lax.broadcasted_iota(jnp.int32, (T, T), 0)
    c_i = lax.broadcasted_iota(jnp.int32, (T, T), 1)
    same = _same_block(r_i, c_i, GDN_CHUNK)
    m_f = jnp.where(same & (c_i <= r_i), 1.0, 0.0).astype(BF16)
    m_b = jnp.where(same & (c_i >= r_i), 1.0, 0.0).astype(BF16)
    m_t = jnp.where(same, 1.0, 0.0).astype(BF16)
    parts = _split3(gb)
    cum_f = sum(jnp.dot(m_f, p, preferred_element_type=F32) for p in parts)
    cum_b = sum(jnp.dot(m_b, p, preferred_element_type=F32) for p in parts)
    tot = sum(jnp.dot(m_t, p, preferred_element_type=F32) for p in parts)
    lane = lax.broadcasted_iota(jnp.int32, gb.shape, 1)
    col_ref[...] = jnp.where(lane < 4, cum_f,
                             jnp.where(lane < 8, cum_b,
                                       jnp.where(lane < 16, gb, tot)))


def _gdn_prep(proj, gb, conv_w):
    B, S, _ = proj.shape
    T = min(512, S)
    nb = T // HALO
    n_halo = S // HALO
    wq = 3 * W_BR
    return pl.pallas_call(
        _gdn_prep_kernel,
        out_shape=(jax.ShapeDtypeStruct((B, S, wq), F32),
                   jax.ShapeDtypeStruct((B, S, LANES), F32)),
        grid=(B, S // T),
        in_specs=[
            pl.BlockSpec((None, HALO, wq), lambda b, i: (b, jnp.maximum(i * nb - 1, 0), 0)),
            pl.BlockSpec((None, T, wq), lambda b, i: (b, i, 0)),
            pl.BlockSpec((None, HALO, wq), lambda b, i: (b, jnp.minimum((i + 1) * nb, n_halo - 1), 0)),
            pl.BlockSpec((None, T, LANES), lambda b, i: (b, i, 0)),
            pl.BlockSpec((GDN_CONV, wq), lambda b, i: (0, 0)),
        ],
        out_specs=(pl.BlockSpec((None, T, wq), lambda b, i: (b, i, 0)),
                   pl.BlockSpec((None, T, LANES), lambda b, i: (b, i, 0))),
        compiler_params=_cparams("parallel", "parallel"),
        name="gdn_prep",
    )(proj, proj, proj, gb, conv_w)


def _gdn_masks(reverse):
    n = N_HEADS * GDN_CHUNK
    r_i = lax.broadcasted_iota(jnp.int32, (n, n), 0)
    c_i = lax.broadcasted_iota(jnp.int32, (n, n), 1)
    same_head = _same_block(r_i, c_i, GDN_CHUNK)
    same_blk = _same_block(r_i, c_i, 8)
    if reverse:
        incl = same_head & (c_i >= r_i)
        strict = same_head & (c_i > r_i)
    else:
        incl = same_head & (c_i <= r_i)
        strict = same_head & (c_i < r_i)
    return incl, strict, same_blk


def _stack_heads(x):
    return jnp.concatenate([x[:, h * D_HEAD:(h + 1) * D_HEAD] for h in range(N_HEADS)], axis=0)


def _mm(a, b):
    return jnp.dot(a, b, preferred_element_type=F32)


def _bf(xs):
    return [x.astype(BF16) for x in xs]


def _unit_tri_solve(a_mats, rhss, same_blks):
    n = a_mats[0].shape[0]
    eye = jnp.where(lax.broadcasted_iota(jnp.int32, (n, n), 0)
                    == lax.broadcasted_iota(jnp.int32, (n, n), 1), 1.0, 0.0)
    d1 = [jnp.where(m, a, 0.0) for a, m in zip(a_mats, same_blks)]
    lo_rhs = _bf([jnp.concatenate([a - d, r], axis=1) for a, d, r in zip(a_mats, d1, rhss)])
    d1b = _bf(d1)
    d2 = [_mm(d, d) for d in d1b]
    d2b = _bf(d2)
    d4b = _bf([_mm(d, d) for d in d2b])
    d3 = [_mm(a, b) for a, b in zip(d1b, d2b)]
    p1 = [eye - a + b - c for a, b, c in zip(d1, d2, d3)]
    t8b = _bf([p + _mm(p.astype(BF16), d) for p, d in zip(p1, d4b)])
    by = [_mm(t, w) for t, w in zip(t8b, lo_rhs)]
    b1b = _bf([x[:, :n] for x in by])
    b2b = _bf([_mm(b, b) for b in b1b])
    b4b = _bf([_mm(b, b) for b in b2b])
    z = [x[:, n:] for x in by]
    for bb in (b1b, b2b, b4b):
        sign = -1.0 if bb is b1b else 1.0
        z = [a + sign * _mm(b, a.astype(BF16)) for a, b in zip(z, bb)]
    return z


def _gdn_prepare(q, k, v, col, row, d):
    g_idx = d * N_HEADS
    b_idx = 8 + d * N_HEADS
    t_idx = 16 + d * N_HEADS
    qs, kbs, ks, rhs_v, rhs_k, gcs, kds, e_tot = [], [], [], [], [], [], [], []
    for h in range(N_HEADS):
        sl = slice(h * D_HEAD, (h + 1) * D_HEAD)
        gc = col[:, g_idx + h:g_idx + h + 1]
        bc = col[:, b_idx + h:b_idx + h + 1]
        gt = col[:, t_idx + h:t_idx + h + 1]
        eg = jnp.exp(gc)
        k_h = k[:, sl]
        kb = k_h * bc
        qs.append(q[:, sl] * eg)
        kbs.append(kb)
        ks.append(k_h)
        rhs_v.append(v[:, sl] * bc)
        rhs_k.append(kb * eg)
        gcs.append(jnp.broadcast_to(gc, (GDN_CHUNK, N_HEADS * GDN_CHUNK)))
        kds.append((k_h * jnp.exp(gt - gc)).astype(BF16))
        e_tot.append(jnp.exp(row[8 + g_idx + h:8 + g_idx + h + 1, :D_HEAD]))
    return dict(
        gram_lhs=jnp.concatenate(kbs + [_stack_heads(q)], axis=0).astype(BF16),
        k_st=jnp.concatenate(ks, axis=0).astype(BF16),
        dlog=jnp.concatenate(gcs, axis=0) - row[d:d + 1, :],
        rhs=jnp.concatenate([jnp.concatenate(rhs_v, axis=0), jnp.concatenate(rhs_k, axis=0)], axis=1),
        qs=jnp.concatenate(qs, axis=0), kds=kds, e_tot=e_tot)


def _gdn_affine_terms(chains, masks):
    n = N_HEADS * GDN_CHUNK
    gram = [lax.dot_general(c["gram_lhs"], c["k_st"], (((1,), (1,)), ((), ())),
                            preferred_element_type=F32) for c in chains]
    decay = [jnp.exp(jnp.where(m[0], c["dlog"], NEG)) for c, m in zip(chains, masks)]
    a_mats = [jnp.where(m[1], g[:n] * dc, 0.0) for g, dc, m in zip(gram, decay, masks)]
    attn_b = _bf([g[n:] * dc for g, dc in zip(gram, decay)])
    xb = _bf(_unit_tri_solve(a_mats, [c["rhs"] for c in chains], [m[2] for m in masks]))
    ax = [_mm(a, x) for a, x in zip(attn_b, xb)]
    out = []
    for c, x, a in zip(chains, xb, ax):
        heads = []
        for h in range(N_HEADS):
            rs = slice(h * GDN_CHUNK, (h + 1) * GDN_CHUNK)
            kx = lax.dot_general(c["kds"][h], x[rs], (((0,), (0,)), ((), ())),
                                 preferred_element_type=F32)
            p = c["qs"][rs] - a[rs, D_HEAD:]
            heads.append(dict(lhs=jnp.concatenate([kx[:, D_HEAD:], p], axis=0).astype(BF16),
                              n=kx[:, :D_HEAD], r=a[rs, :D_HEAD], e=c["e_tot"][h]))
        out.append(heads)
    return out


def _gdn_kernel(qf_ref, kf_ref, vf_ref, colf_ref, rowf_ref,
                qb_ref, kb_ref, vb_ref, colb_ref, rowb_ref,
                of_ref, ob_ref, s_ref):
    nc = qf_ref.shape[0]

    @pl.when(pl.program_id(1) == 0)
    def _():
        s_ref[...] = jnp.zeros(s_ref.shape, F32)

    masks_f = _gdn_masks(False)
    masks_b = _gdn_masks(True)
    chains, masks = [], []
    for c in range(nc):
        cb = nc - 1 - c
        chains.append(_gdn_prepare(qf_ref[c], kf_ref[c], vf_ref[c], colf_ref[c], rowf_ref[c], 0))
        chains.append(_gdn_prepare(qb_ref[cb], kb_ref[cb], vb_ref[cb], colb_ref[cb], rowb_ref[cb], 1))
        masks += [masks_f, masks_b]
    terms = _gdn_affine_terms(chains, masks)

    state = [[s_ref[d, h] for h in range(N_HEADS)] for d in range(2)]
    for c in range(nc):
        ls = [[_mm(terms[2 * c + d][h]["lhs"], state[d][h].astype(BF16)) for h in range(N_HEADS)]
              for d in range(2)]
        for d, o_ref, idx in ((0, of_ref, c), (1, ob_ref, nc - 1 - c)):
            outs = []
            for h in range(N_HEADS):
                t = terms[2 * c + d][h]
                state[d][h] = state[d][h] * t["e"] - ls[d][h][:D_HEAD] + t["n"]
                outs.append(ls[d][h][D_HEAD:] + t["r"])
            o_ref[idx] = jnp.concatenate(outs, axis=1)
    for d in range(2):
        for h in range(N_HEADS):
            s_ref[d, h] = state[d][h]


def _gdn(qkv, col, row):
    B, N, C, _ = qkv.shape
    nc = min(4, N)
    nblk = N // nc
    fwd = lambda j: (lambda b, i: (b, i, 0, j))
    bwd = lambda j: (lambda b, i: (b, nblk - 1 - i, 0, j))
    specs = []
    for mk in (fwd, bwd):
        specs += [pl.BlockSpec((None, nc, C, W_BR), mk(0)),
                  pl.BlockSpec((None, nc, C, W_BR), mk(1)),
                  pl.BlockSpec((None, nc, C, W_BR), mk(2)),
                  pl.BlockSpec((None, nc, C, LANES), mk(0)),
                  pl.BlockSpec((None, nc, 16, 2 * LANES), mk(0))]
    out_sds = jax.ShapeDtypeStruct((B, N, C, W_BR), F32)
    return pl.pallas_call(
        _gdn_kernel,
        out_shape=(out_sds, out_sds),
        grid=(B, nblk),
        in_specs=specs,
        out_specs=(pl.BlockSpec((None, nc, C, W_BR), fwd(0)),
                   pl.BlockSpec((None, nc, C, W_BR), bwd(0))),
        scratch_shapes=[pltpu.VMEM((2, N_HEADS, D_HEAD, D_HEAD), F32)],
        compiler_params=_cparams("parallel", "arbitrary"),
        name="gdn",
    )(qkv, qkv, qkv, col, row, qkv, qkv, qkv, col, row)


def _gdn_post_kernel(of_ref, ob_ref, z_ref, nw_ref, y_ref):
    o = of_ref[...] + ob_ref[...]
    z = z_ref[...].astype(F32)
    for h in range(N_HEADS):
        sl = slice(h * D_HEAD, (h + 1) * D_HEAD)
        t = o[:, sl]
        r = lax.rsqrt(jnp.mean(t * t, axis=-1, keepdims=True) + EPS)
        y_ref[:, sl] = (t * r * nw_ref[...] * _silu(z[:, sl])).astype(y_ref.dtype)


def _gdn_post(o_f, o_b, proj, nw):
    B, S, _ = o_f.shape
    T = min(1024, S)
    return pl.pallas_call(
        _gdn_post_kernel,
        out_shape=jax.ShapeDtypeStruct((B, S, W_BR), BF16),
        grid=(B, S // T),
        in_specs=[
            pl.BlockSpec((None, T, W_BR), lambda b, i: (b, i, 0)),
            pl.BlockSpec((None, T, W_BR), lambda b, i: (b, i, 0)),
            pl.BlockSpec((None, T, W_BR), lambda b, i: (b, i, COL_A_Z)),
            pl.BlockSpec((1, D_HEAD), lambda b, i: (0, 0)),
        ],
        out_specs=pl.BlockSpec((None, T, W_BR), lambda b, i: (b, i, 0)),
        compiler_params=_cparams("parallel", "parallel"),
        name="gdn_post",
    )(o_f, o_b, proj, nw)


def _rope_kernel(q_ref, k_ref, c_ref, sa_ref, sb_ref, qo_ref, ko_ref):
    c = c_ref[...]
    sa = sa_ref[...]
    sb = sb_ref[...]
    half = ROT_DIM // 2
    for src, dst, scale in ((q_ref, qo_ref, DQK ** -0.5), (k_ref, ko_ref, 1.0)):
        for h in range(N_HEADS):
            sl = slice(h * D_HEAD, (h + 1) * D_HEAD)
            x = src[:, sl].astype(F32)
            y = (x * c + pltpu.roll(x, half, 1) * sa + pltpu.roll(x, D_HEAD - half, 1) * sb)
            dst[:, sl] = (y * scale).astype(dst.dtype)


def _rope(proj, cos_t, sin_a, sin_b):
    B, S, _ = proj.shape
    T = min(1024, S)
    sds = jax.ShapeDtypeStruct((B, S, W_BR), BF16)
    tab = pl.BlockSpec((T, D_HEAD), lambda b, i: (i, 0))
    return pl.pallas_call(
        _rope_kernel,
        out_shape=(sds, sds),
        grid=(B, S // T),
        in_specs=[
            pl.BlockSpec((None, T, W_BR), lambda b, i: (b, i, COL_B_Q)),
            pl.BlockSpec((None, T, W_BR), lambda b, i: (b, i, COL_B_K)),
            tab, tab, tab,
        ],
        out_specs=(pl.BlockSpec((None, T, W_BR), lambda b, i: (b, i, 0)),
                   pl.BlockSpec((None, T, W_BR), lambda b, i: (b, i, 0))),
        compiler_params=_cparams("parallel", "parallel"),
        name="rope",
    )(proj, proj, cos_t, sin_a, sin_b)


def _diff_attn_kernel(q_ref, k_ref, v_ref, z_ref, lp_ref, nw_ref, y_ref, *, lambda_init):
    q = q_ref[...]
    k = k_ref[...]
    lane = lax.broadcasted_iota(jnp.int32, q.shape, 1)
    zero = jnp.zeros_like(q)
    lp = lp_ref[...]
    lam = (jnp.exp(jnp.sum(lp[0:1] * lp[1:2], axis=-1, keepdims=True))
           - jnp.exp(jnp.sum(lp[2:3] * lp[3:4], axis=-1, keepdims=True)) + lambda_init)
    probs = []
    for first in (True, False):
        qm = jnp.where((lane < DQK) == first, q, zero)
        s = _dot_nt(qm, k)
        p = jnp.exp(s - jnp.max(s, axis=-1, keepdims=True))
        inv = 1.0 / jnp.sum(p, axis=-1, keepdims=True)
        probs.append(p * (inv if first else inv * lam))
    p = (probs[0] - probs[1]).astype(BF16)
    o = jnp.dot(p, v_ref[...], preferred_element_type=F32)
    r = lax.rsqrt(jnp.mean(o * o, axis=-1, keepdims=True) + EPS)
    o = o * r * nw_ref[...] * (1.0 - lambda_init)
    y_ref[...] = (o * _silu(z_ref[...].astype(F32))).astype(y_ref.dtype)


def _diff_attn(q_rot, k_rot, proj, lam_p, nw, lambda_init):
    B, S, _ = q_rot.shape
    tq = min(256, S)
    nh = W_BR // D_HEAD
    return pl.pallas_call(
        functools.partial(_diff_attn_kernel, lambda_init=lambda_init),
        out_shape=jax.ShapeDtypeStruct((B, S, W_BR), BF16),
        grid=(B, N_HEADS, S // tq),
        in_specs=[
            pl.BlockSpec((None, tq, D_HEAD), lambda b, h, i: (b, i, h)),
            pl.BlockSpec((None, S, D_HEAD), lambda b, h, i: (b, 0, h)),
            pl.BlockSpec((None, S, D_HEAD), lambda b, h, i: (b, 0, COL_B_V * nh + h)),
            pl.BlockSpec((None, tq, D_HEAD), lambda b, h, i: (b, i, COL_B_Z * nh + h)),
            pl.BlockSpec((4, DQK), lambda b, h, i: (0, 0)),
            pl.BlockSpec((1, D_HEAD), lambda b, h, i: (0, 0)),
        ],
        out_specs=pl.BlockSpec((None, tq, D_HEAD), lambda b, h, i: (b, i, h)),
        compiler_params=_cparams("parallel", "parallel", "parallel"),
        name="diff_attn",
    )(q_rot, k_rot, proj, proj, lam_p, nw)


def _conv_mem_kernel(cb_ref, cc_ref, cx_ref, cz_ref, ccp_ref, cxp_ref, ccn_ref, cxn_ref,
                     mq_ref, mz_ref, kv_ref, cw_ref, y_ref):
    i = pl.program_id(1)
    n_i = pl.num_programs(1)
    T = cb_ref.shape[0]
    prev = jnp.where(i > 0, ccp_ref[HALO - 8:, :].astype(F32) * cxp_ref[HALO - 8:, :].astype(F32), 0.0)
    nxt = jnp.where(i < n_i - 1, ccn_ref[:8, :].astype(F32) * cxn_ref[:8, :].astype(F32), 0.0)
    cur = cc_ref[...].astype(F32) * cx_ref[...].astype(F32)
    xe = jnp.concatenate([prev, cur, nxt], axis=0)
    pad = CONV_W // 2
    acc = xe[8 - pad:8 - pad + T, :] * cw_ref[0:1, :]
    for d in range(1, CONV_W):
        acc = acc + xe[8 - pad + d:8 - pad + d + T, :] * cw_ref[d:d + 1, :]
    y_c = cb_ref[...].astype(F32) * acc * _silu(cz_ref[...].astype(F32))
    y_ref[:, :W_BR] = y_c.astype(y_ref.dtype)

    mz = mz_ref[...].astype(F32)
    for h in range(N_HEADS):
        sl = slice(h * D_HEAD, (h + 1) * D_HEAD)
        s = _dot_nt(mq_ref[:, sl], kv_ref[:, sl]) * (D_HEAD ** -0.5)
        p = jnp.exp(s - jnp.max(s, axis=-1, keepdims=True))
        p = (p * (1.0 / jnp.sum(p, axis=-1, keepdims=True))).astype(BF16)
        o = jnp.dot(p, kv_ref[:, W_BR + h * D_HEAD:W_BR + (h + 1) * D_HEAD],
                    preferred_element_type=F32)
        y_ref[:, W_BR + h * D_HEAD:W_BR + (h + 1) * D_HEAD] = (o * _silu(mz[:, sl])).astype(y_ref.dtype)


def _conv_mem(proj, kv, conv_w):
    B, S, _ = proj.shape
    M = kv.shape[1]
    T = min(512, S)
    nb = T // HALO
    n_halo = S // HALO
    main = lambda c: pl.BlockSpec((None, T, W_BR), lambda b, i: (b, i, c))
    prev = lambda c: pl.BlockSpec((None, HALO, W_BR), lambda b, i: (b, jnp.maximum(i * nb - 1, 0), c))
    nxt = lambda c: pl.BlockSpec((None, HALO, W_BR),
                                 lambda b, i: (b, jnp.minimum((i + 1) * nb, n_halo - 1), c))
    return pl.pallas_call(
        _conv_mem_kernel,
        out_shape=jax.ShapeDtypeStruct((B, S, 2 * W_BR), BF16),
        grid=(B, S // T),
        in_specs=[main(COL_C_B), main(COL_C_C), main(COL_C_X), main(COL_C_Z),
                  prev(COL_C_C), prev(COL_C_X), nxt(COL_C_C), nxt(COL_C_X),
                  main(COL_M_Q), main(COL_M_Z),
                  pl.BlockSpec((None, M, 2 * W_BR), lambda b, i: (b, 0, 0)),
                  pl.BlockSpec((CONV_W, W_BR), lambda b, i: (0, 0))],
        out_specs=pl.BlockSpec((None, T, 2 * W_BR), lambda b, i: (b, i, 0)),
        compiler_params=_cparams("parallel", "parallel"),
        name="conv_mem",
    )(proj, proj, proj, proj, proj, proj, proj, proj, proj, proj, kv, conv_w)


def _out_proj_kernel(ya_ref, yb_ref, ycm_ref, w_ref, x_ref, nw_ref, o_ref):
    y = jnp.dot(ya_ref[...], w_ref[0:W_BR, :], preferred_element_type=F32)
    y = y + jnp.dot(yb_ref[...], w_ref[W_BR:2 * W_BR, :], preferred_element_type=F32)
    y = y + jnp.dot(ycm_ref[...], w_ref[2 * W_BR:, :], preferred_element_type=F32)
    r = lax.rsqrt(jnp.mean(y * y, axis=-1, keepdims=True) + EPS)
    o_ref[...] = x_ref[...] + y * r * nw_ref[...]


def _out_proj(y_a, y_b, y_cm, w_out, x, nw):
    B, S, D = x.shape
    T = min(512, S)
    return pl.pallas_call(
        _out_proj_kernel,
        out_shape=jax.ShapeDtypeStruct((B, S, D), F32),
        grid=(B, S // T),
        in_specs=[
            pl.BlockSpec((None, T, W_BR), lambda b, i: (b, i, 0)),
            pl.BlockSpec((None, T, W_BR), lambda b, i: (b, i, 0)),
            pl.BlockSpec((None, T, 2 * W_BR), lambda b, i: (b, i, 0)),
            pl.BlockSpec((4 * W_BR, D), lambda b, i: (0, 0)),
            pl.BlockSpec((None, T, D), lambda b, i: (b, i, 0)),
            pl.BlockSpec((1, D), lambda b, i: (0, 0)),
        ],
        out_specs=pl.BlockSpec((None, T, D), lambda b, i: (b, i, 0)),
        compiler_params=_cparams("parallel", "parallel"),
        name="out_proj",
    )(y_a, y_b, y_cm, w_out, x, nw)


def _rope_tables(S):
    half = ROT_DIM // 2
    inv = ROPE_THETA ** (-jnp.arange(0, ROT_DIM, 2, dtype=F32) / ROT_DIM)
    ang = jnp.arange(S, dtype=F32)[:, None] * inv[None, :]
    cos, sin = jnp.cos(ang), jnp.sin(ang)
    ones = jnp.ones((S, DQK - ROT_DIM), F32)
    zeros = jnp.zeros((S, DQK - ROT_DIM), F32)
    z8 = jnp.zeros((S, half), F32)
    c_map = jnp.concatenate([cos, cos, ones], axis=1)
    sa_map = jnp.concatenate([z8, sin, zeros], axis=1)
    sb_map = jnp.concatenate([-sin, z8, zeros], axis=1)
    two = lambda t: jnp.concatenate([t, t], axis=1)
    return two(c_map), two(sa_map), two(sb_map)


def _row_info(col, N):
    B = col.shape[0]
    c4 = col.reshape(B, N, GDN_CHUNK, LANES)
    g_st = jnp.swapaxes(c4[..., 0:8], 2, 3).reshape(B, N, 2, N_HEADS * GDN_CHUNK)
    tot = jnp.broadcast_to(c4[:, :, 0, 16:24][..., None], (B, N, 8, N_HEADS * GDN_CHUNK))
    pad = jnp.zeros((B, N, 6, N_HEADS * GDN_CHUNK), F32)
    return jnp.concatenate([g_st, pad, tot], axis=2)


def _prep_layer_weights(l, norm_pre, norm_post, norm_mem, w_in, gdn_conv, gdn_A_log, gdn_dt_bias,
                        gdn_norm, diff_lambda, diff_norm, conv_w, w_mem_kv, w_out):
    w = w_in[l]
    q_end = 3 * W_BR
    n_db = 2 * N_HEADS
    dec = w[:, q_end:q_end + n_db]
    bet = w[:, q_end + n_db:q_end + 2 * n_db]
    w_main = jnp.concatenate([w[:, :q_end], w[:, q_end + 2 * n_db:]], axis=1).astype(BF16)
    w_gb = jnp.concatenate([dec, bet, dec, jnp.zeros((w.shape[0], LANES - 3 * n_db), F32)],
                           axis=1).astype(BF16)
    lane_pad = lambda t: jnp.concatenate(
        [t.reshape(1, n_db), jnp.zeros((1, n_db), F32), t.reshape(1, n_db),
         jnp.zeros((1, LANES - 3 * n_db), F32)], axis=1)
    return dict(
        norm_pre=norm_pre[l][None, :], norm_post=norm_post[l][None, :], norm_mem=norm_mem[l][None, :],
        w_main=w_main, w_gb=w_gb, alog=lane_pad(gdn_A_log[l]), dtb=lane_pad(gdn_dt_bias[l]),
        gdn_conv=gdn_conv[l], gdn_norm=gdn_norm[l][None, :], diff_lambda=diff_lambda[l],
        diff_norm=diff_norm[l][None, :], conv_w=conv_w[l], w_kv=w_mem_kv[l].astype(BF16),
        w_out=w_out[l].astype(BF16))


def _layer(x, mem, p, lambda_init, tables):
    B, S, _ = x.shape
    N = S // GDN_CHUNK
    proj, gb = _in_proj(x, p["norm_pre"], p["w_main"], p["w_gb"], p["alog"], p["dtb"])
    kv = _kv_proj(mem, p["norm_mem"], p["w_kv"])
    qkv, col = _gdn_prep(proj, gb, p["gdn_conv"])
    row = _row_info(col, N)
    o_f, o_b = _gdn(qkv.reshape(B, N, GDN_CHUNK, 3 * W_BR), col.reshape(B, N, GDN_CHUNK, LANES), row)
    y_a = _gdn_post(o_f.reshape(B, S, W_BR), o_b.reshape(B, S, W_BR), proj, p["gdn_norm"])
    q_rot, k_rot = _rope(proj, *tables)
    y_b = _diff_attn(q_rot, k_rot, proj, p["diff_lambda"], p["diff_norm"], lambda_init)
    y_cm = _conv_mem(proj, kv, p["conv_w"])
    return _out_proj(y_a, y_b, y_cm, p["w_out"], x, p["norm_post"])


def _trunk(x, mem, layer_params):
    tables = _rope_tables(x.shape[1])
    for l, p in enumerate(layer_params):
        lambda_init = 0.8 - 0.6 * math.exp(-0.3 * l)
        x = _layer(x, mem, p, lambda_init, tables)
    return x


def kernel(x_prompt, x_sample, mem_prompt, mem_sample, norm_pre, norm_post, norm_mem, w_in, gdn_conv,
           gdn_A_log, gdn_dt_bias, gdn_norm, diff_lambda, diff_norm, conv_w, w_mem_kv, w_out):
    depth = w_in.shape[0]
    params = [_prep_layer_weights(l, norm_pre, norm_post, norm_mem, w_in, gdn_conv, gdn_A_log,
                                  gdn_dt_bias, gdn_norm, diff_lambda, diff_norm, conv_w, w_mem_kv, w_out)
              for l in range(depth)]
    return (_trunk(x_prompt, mem_prompt, params), _trunk(x_sample, mem_sample, params))
```

```python
import functools
import math

import jax
import jax.numpy as jnp
from jax import lax
from jax.experimental import pallas as pl
from jax.experimental.pallas import tpu as pltpu

F32 = jnp.float32
BF16 = jnp.bfloat16
EPS = 1e-6

W_BR = 512
N_HEADS = 4
D_HEAD = 128
DQK = 64
ROT_DIM = 16
ROPE_THETA = 500000.0
GDN_CONV = 5
GDN_CHUNK = 64
CONV_W = 3
N_MAIN = 14 * W_BR
LANES = 128
HALO = 16
VT_ROWS = D_HEAD + 16
DIFF_SUB = 256
DIFF_KEYS = 512
NEG = -1e30
VMEM_LIMIT = 56 * 1024 * 1024

COL_A_Z = 3
COL_B_Q, COL_B_K, COL_B_V, COL_B_Z = 4, 5, 6, 7
COL_C_B, COL_C_C, COL_C_X, COL_C_Z = 8, 9, 10, 11
COL_M_Q, COL_M_Z = 12, 13


def _cparams(*sem):
    return pltpu.CompilerParams(dimension_semantics=sem, vmem_limit_bytes=VMEM_LIMIT)


def _sigmoid(x):
    return 1.0 / (1.0 + jnp.exp(-x))


def _silu(x):
    return x * _sigmoid(x)


def _softplus(x):
    return jnp.maximum(x, 0.0) + jnp.log(1.0 + jnp.exp(-jnp.abs(x)))


def _dot(a, b):
    return jnp.dot(a.astype(BF16), b.astype(BF16), preferred_element_type=F32)


def _dot_nt(a, b):
    return lax.dot_general(a.astype(BF16), b.astype(BF16), (((1,), (1,)), ((), ())),
                           preferred_element_type=F32)


def _dot_tn(a, b):
    return lax.dot_general(a.astype(BF16), b.astype(BF16), (((0,), (0,)), ((), ())),
                           preferred_element_type=F32)


def _same_block(r_i, c_i, size):
    shift = size.bit_length() - 1
    return jnp.right_shift(r_i, shift) == jnp.right_shift(c_i, shift)


def _split3(x):
    hi = x.astype(BF16)
    r1 = x - hi.astype(F32)
    mid = r1.astype(BF16)
    lo = (r1 - mid.astype(F32)).astype(BF16)
    return hi, mid, lo


def _in_proj_kernel(x_ref, nw_ref, w_ref, wgb_ref, alog_ref, dtb_ref, o_ref, gb_ref, h_ref):
    @pl.when(pl.program_id(2) == 0)
    def _():
        x = x_ref[...]
        ms = jnp.mean(x * x, axis=-1, keepdims=True)
        h = (x * lax.rsqrt(ms + EPS) * nw_ref[...]).astype(BF16)
        h_ref[...] = h
        raw = jnp.dot(h, wgb_ref[...], preferred_element_type=F32)
        lane = lax.broadcasted_iota(jnp.int32, raw.shape, 1)
        g = -jnp.exp(alog_ref[...]) * _softplus(raw + dtb_ref[...])
        is_beta = (lane >= 8) & (lane < 16)
        gb_ref[...] = jnp.where(is_beta, _sigmoid(raw), g)

    o_ref[...] = jnp.dot(h_ref[...], w_ref[...], preferred_element_type=F32).astype(o_ref.dtype)


def _in_proj(x, nw, w_main, w_gb, alog, dtb):
    B, S, D = x.shape
    tm = min(1024, S)
    tn = 1024
    return pl.pallas_call(
        _in_proj_kernel,
        out_shape=(jax.ShapeDtypeStruct((B, S, N_MAIN), BF16),
                   jax.ShapeDtypeStruct((B, S, LANES), F32)),
        grid=(B, S // tm, N_MAIN // tn),
        in_specs=[
            pl.BlockSpec((None, tm, D), lambda b, i, j: (b, i, 0)),
            pl.BlockSpec((1, D), lambda b, i, j: (0, 0)),
            pl.BlockSpec((D, tn), lambda b, i, j: (0, j)),
            pl.BlockSpec((D, LANES), lambda b, i, j: (0, 0)),
            pl.BlockSpec((1, LANES), lambda b, i, j: (0, 0)),
            pl.BlockSpec((1, LANES), lambda b, i, j: (0, 0)),
        ],
        out_specs=(pl.BlockSpec((None, tm, tn), lambda b, i, j: (b, i, j)),
                   pl.BlockSpec((None, tm, LANES), lambda b, i, j: (b, i, 0))),
        scratch_shapes=[pltpu.VMEM((tm, D), BF16)],
        compiler_params=_cparams("parallel", "parallel", "arbitrary"),
        name="in_proj",
    )(x, nw, w_main, w_gb, alog, dtb)


def _kv_proj_kernel(x_ref, nw_ref, w_ref, o_ref):
    x = x_ref[...]
    ms = jnp.mean(x * x, axis=-1, keepdims=True)
    h = (x * lax.rsqrt(ms + EPS) * nw_ref[...]).astype(BF16)
    o_ref[...] = jnp.dot(h, w_ref[...], preferred_element_type=F32).astype(o_ref.dtype)


def _kv_proj(mem, nw, w_kv):
    B, M, D = mem.shape
    n_out = w_kv.shape[1]
    return pl.pallas_call(
        _kv_proj_kernel,
        out_shape=jax.ShapeDtypeStruct((B, M, n_out), BF16),
        grid=(B,),
        in_specs=[
            pl.BlockSpec((None, M, D), lambda b: (b, 0, 0)),
            pl.BlockSpec((1, D), lambda b: (0, 0)),
            pl.BlockSpec((D, n_out), lambda b: (0, 0)),
        ],
        out_specs=pl.BlockSpec((None, M, n_out), lambda b: (b, 0, 0)),
        compiler_params=_cparams("parallel"),
        name="kv_proj",
    )(mem, nw, w_kv)


def _gdn_prep_kernel(xp_ref, x_ref, xn_ref, gb_ref, cw_ref, qkv_ref, col_ref):
    i = pl.program_id(1)
    n_i = pl.num_programs(1)
    T = x_ref.shape[0]
    pad = GDN_CONV // 2
    prev = jnp.where(i > 0, xp_ref[HALO - 8:, :].astype(F32), 0.0)
    nxt = jnp.where(i < n_i - 1, xn_ref[:8, :].astype(F32), 0.0)
    xe = jnp.concatenate([prev, x_ref[...].astype(F32), nxt], axis=0)
    acc = xe[8 - pad:8 - pad + T, :] * cw_ref[0:1, :]
    for d in range(1, GDN_CONV):
        acc = acc + xe[8 - pad + d:8 - pad + d + T, :] * cw_ref[d:d + 1, :]
    y = _silu(acc)
    for h in range(2 * N_HEADS):
        sl = slice(h * D_HEAD, (h + 1) * D_HEAD)
        t = y[:, sl]
        r = lax.rsqrt(jnp.sum(t * t, axis=-1, keepdims=True) + EPS)
        if h < N_HEADS:
            r = r * (D_HEAD ** -0.5)
        qkv_ref[:, sl] = t * r
    qkv_ref[:, 2 * W_BR:] = y[:, 2 * W_BR:]

    gb = gb_ref[...]
    r_i = lax.broadcasted_iota(jnp.int32, (T, T), 0)
    c_i = lax.broadcasted_iota(jnp.int32, (T, T), 1)
    same = _same_block(r_i, c_i, GDN_CHUNK)
    m_f = jnp.where(same & (c_i <= r_i), 1.0, 0.0).astype(BF16)
    m_b = jnp.where(same & (c_i >= r_i), 1.0, 0.0).astype(BF16)
    m_t = jnp.where(same, 1.0, 0.0).astype(BF16)
    parts = _split3(gb)
    cum_f = sum(jnp.dot(m_f, p, preferred_element_type=F32) for p in parts)
    cum_b = sum(jnp.dot(m_b, p, preferred_element_type=F32) for p in parts)
    tot = sum(jnp.dot(m_t, p, preferred_element_type=F32) for p in parts)
    lane = lax.broadcasted_iota(jnp.int32, gb.shape, 1)
    col_ref[...] = jnp.where(lane < 4, cum_f,
                             jnp.where(lane < 8, cum_b,
                                       jnp.where(lane < 16, gb, tot)))


def _gdn_prep(proj, gb, conv_w):
    B, S, _ = proj.shape
    T = min(512, S)
    nb = T // HALO
    n_halo = S // HALO
    wq = 3 * W_BR
    return pl.pallas_call(
        _gdn_prep_kernel,
        out_shape=(jax.ShapeDtypeStruct((B, S, wq), F32),
                   jax.ShapeDtypeStruct((B, S, LANES), F32)),
        grid=(B, S // T),
        in_specs=[
            pl.BlockSpec((None, HALO, wq), lambda b, i: (b, jnp.maximum(i * nb - 1, 0), 0)),
            pl.BlockSpec((None, T, wq), lambda b, i: (b, i, 0)),
            pl.BlockSpec((None, HALO, wq), lambda b, i: (b, jnp.minimum((i + 1) * nb, n_halo - 1), 0)),
            pl.BlockSpec((None, T, LANES), lambda b, i: (b, i, 0)),
            pl.BlockSpec((GDN_CONV, wq), lambda b, i: (0, 0)),
        ],
        out_specs=(pl.BlockSpec((None, T, wq), lambda b, i: (b, i, 0)),
                   pl.BlockSpec((None, T, LANES), lambda b, i: (b, i, 0))),
        compiler_params=_cparams("parallel", "parallel"),
        name="gdn_prep",
    )(proj, proj, proj, gb, conv_w)


def _gdn_masks(reverse):
    n = N_HEADS * GDN_CHUNK
    r_i = lax.broadcasted_iota(jnp.int32, (n, n), 0)
    c_i = lax.broadcasted_iota(jnp.int32, (n, n), 1)
    same_head = _same_block(r_i, c_i, GDN_CHUNK)
    same_blk = _same_block(r_i, c_i, 8)
    if reverse:
        incl = same_head & (c_i >= r_i)
        strict = same_head & (c_i > r_i)
    else:
        incl = same_head & (c_i <= r_i)
        strict = same_head & (c_i < r_i)
    return incl, strict, same_blk


def _stack_heads(x):
    return jnp.concatenate([x[:, h * D_HEAD:(h + 1) * D_HEAD] for h in range(N_HEADS)], axis=0)


def _mm(a, b):
    return jnp.dot(a, b, preferred_element_type=F32)


def _bf(xs):
    return [x.astype(BF16) for x in xs]


def _unit_tri_solve(a_mats, rhss, same_blks):
    n = a_mats[0].shape[0]
    eye = jnp.where(lax.broadcasted_iota(jnp.int32, (n, n), 0)
                    == lax.broadcasted_iota(jnp.int32, (n, n), 1), 1.0, 0.0)
    d1 = [jnp.where(m, a, 0.0) for a, m in zip(a_mats, same_blks)]
    lo_rhs = _bf([jnp.concatenate([a - d, r], axis=1) for a, d, r in zip(a_mats, d1, rhss)])
    d1b = _bf(d1)
    d2 = [_mm(d, d) for d in d1b]
    d2b = _bf(d2)
    d4b = _bf([_mm(d, d) for d in d2b])
    d3 = [_mm(a, b) for a, b in zip(d1b, d2b)]
    p1 = [eye - a + b - c for a, b, c in zip(d1, d2, d3)]
    t8b = _bf([p + _mm(p.astype(BF16), d) for p, d in zip(p1, d4b)])
    by = [_mm(t, w) for t, w in zip(t8b, lo_rhs)]
    b1b = _bf([x[:, :n] for x in by])
    b2b = _bf([_mm(b, b) for b in b1b])
    b4b = _bf([_mm(b, b) for b in b2b])
    z = [x[:, n:] for x in by]
    for bb in (b1b, b2b, b4b):
        sign = -1.0 if bb is b1b else 1.0
        z = [a + sign * _mm(b, a.astype(BF16)) for a, b in zip(z, bb)]
    return z


def _gdn_prepare(q, k, v, col, row, d):
    g_idx = d * N_HEADS
    b_idx = 8 + d * N_HEADS
    t_idx = 16 + d * N_HEADS
    qs, kbs, ks, rhs_v, rhs_k, gcs, kds, e_tot = [], [], [], [], [], [], [], []
    for h in range(N_HEADS):
        sl = slice(h * D_HEAD, (h + 1) * D_HEAD)
        gc = col[:, g_idx + h:g_idx + h + 1]
        bc = col[:, b_idx + h:b_idx + h + 1]
        gt = col[:, t_idx + h:t_idx + h + 1]
        eg = jnp.exp(gc)
        k_h = k[:, sl]
        kb = k_h * bc
        qs.append(q[:, sl] * eg)
        kbs.append(kb)
        ks.append(k_h)
        rhs_v.append(v[:, sl] * bc)
        rhs_k.append(kb * eg)
        gcs.append(jnp.broadcast_to(gc, (GDN_CHUNK, N_HEADS * GDN_CHUNK)))
        kds.append((k_h * jnp.exp(gt - gc)).astype(BF16))
        e_tot.append(jnp.exp(row[8 + g_idx + h:8 + g_idx + h + 1, :D_HEAD]))
    return dict(
        gram_lhs=jnp.concatenate(kbs + [_stack_heads(q)], axis=0).astype(BF16),
        k_st=jnp.concatenate(ks, axis=0).astype(BF16),
        dlog=jnp.concatenate(gcs, axis=0) - row[d:d + 1, :],
        rhs=jnp.concatenate([jnp.concatenate(rhs_v, axis=0), jnp.concatenate(rhs_k, axis=0)], axis=1),
        qs=jnp.concatenate(qs, axis=0), kds=kds, e_tot=e_tot)


def _gdn_affine_terms(chains, masks):
    n = N_HEADS * GDN_CHUNK
    gram = [lax.dot_general(c["gram_lhs"], c["k_st"], (((1,), (1,)), ((), ())),
                            preferred_element_type=F32) for c in chains]
    decay = [jnp.exp(jnp.where(m[0], c["dlog"], NEG)) for c, m in zip(chains, masks)]
    a_mats = [jnp.where(m[1], g[:n] * dc, 0.0) for g, dc, m in zip(gram, decay, masks)]
    attn_b = _bf([g[n:] * dc for g, dc in zip(gram, decay)])
    xb = _bf(_unit_tri_solve(a_mats, [c["rhs"] for c in chains], [m[2] for m in masks]))
    ax = [_mm(a, x) for a, x in zip(attn_b, xb)]
    out = []
    for c, x, a in zip(chains, xb, ax):
        heads = []
        for h in range(N_HEADS):
            rs = slice(h * GDN_CHUNK, (h + 1) * GDN_CHUNK)
            kx = lax.dot_general(c["kds"][h], x[rs], (((0,), (0,)), ((), ())),
                                 preferred_element_type=F32)
            p = c["qs"][rs] - a[rs, D_HEAD:]
            heads.append(dict(lhs=jnp.concatenate([kx[:, D_HEAD:], p], axis=0).astype(BF16),
                              n=kx[:, :D_HEAD], r=a[rs, :D_HEAD], e=c["e_tot"][h]))
        out.append(heads)
    return out


def _gdn_kernel(qf_ref, kf_ref, vf_ref, colf_ref, rowf_ref,
                qb_ref, kb_ref, vb_ref, colb_ref, rowb_ref,
                of_ref, ob_ref, s_ref):
    nc = qf_ref.shape[0]

    @pl.when(pl.program_id(1) == 0)
    def _():
        s_ref[...] = jnp.zeros(s_ref.shape, F32)

    masks_f = _gdn_masks(False)
    masks_b = _gdn_masks(True)
    chains, masks = [], []
    for c in range(nc):
        cb = nc - 1 - c
        chains.append(_gdn_prepare(qf_ref[c], kf_ref[c], vf_ref[c], colf_ref[c], rowf_ref[c], 0))
        chains.append(_gdn_prepare(qb_ref[cb], kb_ref[cb], vb_ref[cb], colb_ref[cb], rowb_ref[cb], 1))
        masks += [masks_f, masks_b]
    terms = _gdn_affine_terms(chains, masks)

    state = [[s_ref[d, h] for h in range(N_HEADS)] for d in range(2)]
    for c in range(nc):
        ls = [[_mm(terms[2 * c + d][h]["lhs"], state[d][h].astype(BF16)) for h in range(N_HEADS)]
              for d in range(2)]
        for d, o_ref, idx in ((0, of_ref, c), (1, ob_ref, nc - 1 - c)):
            outs = []
            for h in range(N_HEADS):
                t = terms[2 * c + d][h]
                state[d][h] = state[d][h] * t["e"] - ls[d][h][:D_HEAD] + t["n"]
                outs.append(ls[d][h][D_HEAD:] + t["r"])
            o_ref[idx] = jnp.concatenate(outs, axis=1)
    for d in range(2):
        for h in range(N_HEADS):
            s_ref[d, h] = state[d][h]


def _gdn(qkv, col, row):
    B, N, C, _ = qkv.shape
    nc = min(4, N)
    nblk = N // nc
    fwd = lambda j: (lambda b, i: (b, i, 0, j))
    bwd = lambda j: (lambda b, i: (b, nblk - 1 - i, 0, j))
    specs = []
    for mk in (fwd, bwd):
        specs += [pl.BlockSpec((None, nc, C, W_BR), mk(0)),
                  pl.BlockSpec((None, nc, C, W_BR), mk(1)),
                  pl.BlockSpec((None, nc, C, W_BR), mk(2)),
                  pl.BlockSpec((None, nc, C, LANES), mk(0)),
                  pl.BlockSpec((None, nc, 16, 2 * LANES), mk(0))]
    out_sds = jax.ShapeDtypeStruct((B, N, C, W_BR), F32)
    return pl.pallas_call(
        _gdn_kernel,
        out_shape=(out_sds, out_sds),
        grid=(B, nblk),
        in_specs=specs,
        out_specs=(pl.BlockSpec((None, nc, C, W_BR), fwd(0)),
                   pl.BlockSpec((None, nc, C, W_BR), bwd(0))),
        scratch_shapes=[pltpu.VMEM((2, N_HEADS, D_HEAD, D_HEAD), F32)],
        compiler_params=_cparams("parallel", "arbitrary"),
        name="gdn",
    )(qkv, qkv, qkv, col, row, qkv, qkv, qkv, col, row)


def _gdn_post_kernel(of_ref, ob_ref, z_ref, nw_ref, y_ref):
    o = of_ref[...] + ob_ref[...]
    z = z_ref[...].astype(F32)
    for h in range(N_HEADS):
        sl = slice(h * D_HEAD, (h + 1) * D_HEAD)
        t = o[:, sl]
        r = lax.rsqrt(jnp.mean(t * t, axis=-1, keepdims=True) + EPS)
        y_ref[:, sl] = (t * r * nw_ref[...] * _silu(z[:, sl])).astype(y_ref.dtype)


def _gdn_post(o_f, o_b, proj, nw):
    B, S, _ = o_f.shape
    T = min(1024, S)
    return pl.pallas_call(
        _gdn_post_kernel,
        out_shape=jax.ShapeDtypeStruct((B, S, W_BR), BF16),
        grid=(B, S // T),
        in_specs=[
            pl.BlockSpec((None, T, W_BR), lambda b, i: (b, i, 0)),
            pl.BlockSpec((None, T, W_BR), lambda b, i: (b, i, 0)),
            pl.BlockSpec((None, T, W_BR), lambda b, i: (b, i, COL_A_Z)),
            pl.BlockSpec((1, D_HEAD), lambda b, i: (0, 0)),
        ],
        out_specs=pl.BlockSpec((None, T, W_BR), lambda b, i: (b, i, 0)),
        compiler_params=_cparams("parallel", "parallel"),
        name="gdn_post",
    )(o_f, o_b, proj, nw)


def _rope_kernel(q_ref, k_ref, v_ref, c_ref, sa_ref, sb_ref, qo_ref, ko_ref, vt_ref):
    c = c_ref[...]
    sa = sa_ref[...]
    sb = sb_ref[...]
    half = ROT_DIM // 2
    q_scale = DQK ** -0.5 * math.log2(math.e)
    for src, dst, scale in ((q_ref, qo_ref, q_scale), (k_ref, ko_ref, 1.0)):
        for h in range(N_HEADS):
            sl = slice(h * D_HEAD, (h + 1) * D_HEAD)
            x = src[:, sl].astype(F32)
            y = (x * c + pltpu.roll(x, half, 1) * sa + pltpu.roll(x, D_HEAD - half, 1) * sb)
            dst[:, sl] = (y * scale).astype(dst.dtype)
    v_t = v_ref[...].astype(F32).T.astype(vt_ref.dtype)
    ones = jnp.ones((VT_ROWS - D_HEAD, v_t.shape[1]), vt_ref.dtype)
    for h in range(N_HEADS):
        vt_ref[h, :D_HEAD, :] = v_t[h * D_HEAD:(h + 1) * D_HEAD, :]
        vt_ref[h, D_HEAD:, :] = ones


def _rope(proj, cos_t, sin_a, sin_b):
    B, S, _ = proj.shape
    T = min(1024, S)
    sds = jax.ShapeDtypeStruct((B, S, W_BR), BF16)
    tab = pl.BlockSpec((T, D_HEAD), lambda b, i: (i, 0))
    return pl.pallas_call(
        _rope_kernel,
        out_shape=(sds, sds, jax.ShapeDtypeStruct((B, N_HEADS, VT_ROWS, S), BF16)),
        grid=(B, S // T),
        in_specs=[
            pl.BlockSpec((None, T, W_BR), lambda b, i: (b, i, COL_B_Q)),
            pl.BlockSpec((None, T, W_BR), lambda b, i: (b, i, COL_B_K)),
            pl.BlockSpec((None, T, W_BR), lambda b, i: (b, i, COL_B_V)),
            tab, tab, tab,
        ],
        out_specs=(pl.BlockSpec((None, T, W_BR), lambda b, i: (b, i, 0)),
                   pl.BlockSpec((None, T, W_BR), lambda b, i: (b, i, 0)),
                   pl.BlockSpec((None, N_HEADS, VT_ROWS, T), lambda b, i: (b, 0, 0, i))),
        compiler_params=_cparams("parallel", "parallel"),
        name="rope",
    )(proj, proj, proj, cos_t, sin_a, sin_b)


def _diff_attn_kernel(q_ref, k_ref, vt_ref, z_ref, lp_ref, nw_ref, y_ref, *, lambda_init):
    tq = q_ref.shape[0]
    sub = min(DIFF_SUB, tq)
    n_keys = k_ref.shape[0]
    ck = min(DIFF_KEYS, n_keys)
    lane = lax.broadcasted_iota(jnp.int32, (sub, D_HEAD), 1)
    lp = lp_ref[...]
    lam = (jnp.exp(jnp.sum(lp[0:1] * lp[1:2], axis=-1, keepdims=True))
           - jnp.exp(jnp.sum(lp[2:3] * lp[3:4], axis=-1, keepdims=True)) + lambda_init)
    streams = [(j, first) for j in range(tq // sub) for first in (True, False)]
    q_masked = []
    for j, first in streams:
        q = q_ref[j * sub:(j + 1) * sub, :]
        q_masked.append(jnp.where((lane < DQK) == first, q, jnp.zeros_like(q)))

    def scores(c):
        k_c = k_ref[c * ck:(c + 1) * ck, :]
        return [lax.dot_general(k_c, qm, (((1,), (1,)), ((), ())), preferred_element_type=F32)
                for qm in q_masked]

    run_max = [None] * len(streams)
    acc = [None] * len(streams)
    pending = scores(0)
    for c in range(n_keys // ck):
        s_list = pending
        if (c + 1) * ck < n_keys:
            pending = scores(c + 1)
        vt_c = vt_ref[:, c * ck:(c + 1) * ck]
        for t, s in enumerate(s_list):
            m_c = jnp.max(s, axis=0, keepdims=True)
            m_new = m_c if c == 0 else jnp.maximum(run_max[t], m_c)
            p = jnp.exp2(s - m_new).astype(BF16)
            pv = jnp.dot(vt_c, p, preferred_element_type=F32)
            acc[t] = pv if c == 0 else acc[t] * jnp.exp2(run_max[t] - m_new) + pv
            run_max[t] = m_new

    for j in range(tq // sub):
        halves = []
        for first in (True, False):
            o_aug = acc[streams.index((j, first))]
            inv = 1.0 / o_aug[D_HEAD:D_HEAD + 1, :]
            halves.append(o_aug[:D_HEAD, :] * (inv if first else inv * lam))
        o = (halves[0] - halves[1]).T
        r = lax.rsqrt(jnp.mean(o * o, axis=-1, keepdims=True) + EPS)
        o = o * r * nw_ref[...] * (1.0 - lambda_init)
        rows = slice(j * sub, (j + 1) * sub)
        y_ref[rows, :] = (o * _silu(z_ref[rows, :].astype(F32))).astype(y_ref.dtype)


def _diff_attn(q_rot, k_rot, v_t, proj, lam_p, nw, lambda_init):
    B, S, _ = q_rot.shape
    tq = min(2 * DIFF_SUB, S)
    nh = W_BR // D_HEAD
    return pl.pallas_call(
        functools.partial(_diff_attn_kernel, lambda_init=lambda_init),
        out_shape=jax.ShapeDtypeStruct((B, S, W_BR), BF16),
        grid=(B, N_HEADS, S // tq),
        in_specs=[
            pl.BlockSpec((None, tq, D_HEAD), lambda b, h, i: (b, i, h)),
            pl.BlockSpec((None, S, D_HEAD), lambda b, h, i: (b, 0, h)),
            pl.BlockSpec((None, None, VT_ROWS, S), lambda b, h, i: (b, h, 0, 0)),
            pl.BlockSpec((None, tq, D_HEAD), lambda b, h, i: (b, i, COL_B_Z * nh + h)),
            pl.BlockSpec((4, DQK), lambda b, h, i: (0, 0)),
            pl.BlockSpec((1, D_HEAD), lambda b, h, i: (0, 0)),
        ],
        out_specs=pl.BlockSpec((None, tq, D_HEAD), lambda b, h, i: (b, i, h)),
        compiler_params=_cparams("parallel", "parallel", "parallel"),
        name="diff_attn",
    )(q_rot, k_rot, v_t, proj, lam_p, nw)


def _conv_mem_kernel(cb_ref, cc_ref, cx_ref, cz_ref, ccp_ref, cxp_ref, ccn_ref, cxn_ref,
                     mq_ref, mz_ref, kv_ref, cw_ref, y_ref):
    i = pl.program_id(1)
    n_i = pl.num_programs(1)
    T = cb_ref.shape[0]
    prev = jnp.where(i > 0, ccp_ref[HALO - 8:, :].astype(F32) * cxp_ref[HALO - 8:, :].astype(F32), 0.0)
    nxt = jnp.where(i < n_i - 1, ccn_ref[:8, :].astype(F32) * cxn_ref[:8, :].astype(F32), 0.0)
    cur = cc_ref[...].astype(F32) * cx_ref[...].astype(F32)
    xe = jnp.concatenate([prev, cur, nxt], axis=0)
    pad = CONV_W // 2
    acc = xe[8 - pad:8 - pad + T, :] * cw_ref[0:1, :]
    for d in range(1, CONV_W):
        acc = acc + xe[8 - pad + d:8 - pad + d + T, :] * cw_ref[d:d + 1, :]
    y_c = cb_ref[...].astype(F32) * acc * _silu(cz_ref[...].astype(F32))
    y_ref[:, :W_BR] = y_c.astype(y_ref.dtype)

    mz = mz_ref[...].astype(F32)
    for h in range(N_HEADS):
        sl = slice(h * D_HEAD, (h + 1) * D_HEAD)
        s = _dot_nt(mq_ref[:, sl], kv_ref[:, sl]) * (D_HEAD ** -0.5)
        p = jnp.exp(s - jnp.max(s, axis=-1, keepdims=True))
        p = (p * (1.0 / jnp.sum(p, axis=-1, keepdims=True))).astype(BF16)
        o = jnp.dot(p, kv_ref[:, W_BR + h * D_HEAD:W_BR + (h + 1) * D_HEAD],
                    preferred_element_type=F32)
        y_ref[:, W_BR + h * D_HEAD:W_BR + (h + 1) * D_HEAD] = (o * _silu(mz[:, sl])).astype(y_ref.dtype)


def _conv_mem(proj, kv, conv_w):
    B, S, _ = proj.shape
    M = kv.shape[1]
    T = min(512, S)
    nb = T // HALO
    n_halo = S // HALO
    main = lambda c: pl.BlockSpec((None, T, W_BR), lambda b, i: (b, i, c))
    prev = lambda c: pl.BlockSpec((None, HALO, W_BR), lambda b, i: (b, jnp.maximum(i * nb - 1, 0), c))
    nxt = lambda c: pl.BlockSpec((None, HALO, W_BR),
                                 lambda b, i: (b, jnp.minimum((i + 1) * nb, n_halo - 1), c))
    return pl.pallas_call(
        _conv_mem_kernel,
        out_shape=jax.ShapeDtypeStruct((B, S, 2 * W_BR), BF16),
        grid=(B, S // T),
        in_specs=[main(COL_C_B), main(COL_C_C), main(COL_C_X), main(COL_C_Z),
                  prev(COL_C_C), prev(COL_C_X), nxt(COL_C_C), nxt(COL_C_X),
                  main(COL_M_Q), main(COL_M_Z),
                  pl.BlockSpec((None, M, 2 * W_BR), lambda b, i: (b, 0, 0)),
                  pl.BlockSpec((CONV_W, W_BR), lambda b, i: (0, 0))],
        out_specs=pl.BlockSpec((None, T, 2 * W_BR), lambda b, i: (b, i, 0)),
        compiler_params=_cparams("parallel", "parallel"),
        name="conv_mem",
    )(proj, proj, proj, proj, proj, proj, proj, proj, proj, proj, kv, conv_w)


def _out_proj_kernel(ya_ref, yb_ref, ycm_ref, w_ref, x_ref, nw_ref, o_ref):
    y = jnp.dot(ya_ref[...], w_ref[0:W_BR, :], preferred_element_type=F32)
    y = y + jnp.dot(yb_ref[...], w_ref[W_BR:2 * W_BR, :], preferred_element_type=F32)
    y = y + jnp.dot(ycm_ref[...], w_ref[2 * W_BR:, :], preferred_element_type=F32)
    r = lax.rsqrt(jnp.mean(y * y, axis=-1, keepdims=True) + EPS)
    o_ref[...] = x_ref[...] + y * r * nw_ref[...]


def _out_proj(y_a, y_b, y_cm, w_out, x, nw):
    B, S, D = x.shape
    T = min(512, S)
    return pl.pallas_call(
        _out_proj_kernel,
        out_shape=jax.ShapeDtypeStruct((B, S, D), F32),
        grid=(B, S // T),
        in_specs=[
            pl.BlockSpec((None, T, W_BR), lambda b, i: (b, i, 0)),
            pl.BlockSpec((None, T, W_BR), lambda b, i: (b, i, 0)),
            pl.BlockSpec((None, T, 2 * W_BR), lambda b, i: (b, i, 0)),
            pl.BlockSpec((4 * W_BR, D), lambda b, i: (0, 0)),
            pl.BlockSpec((None, T, D), lambda b, i: (b, i, 0)),
            pl.BlockSpec((1, D), lambda b, i: (0, 0)),
        ],
        out_specs=pl.BlockSpec((None, T, D), lambda b, i: (b, i, 0)),
        compiler_params=_cparams("parallel", "parallel"),
        name="out_proj",
    )(y_a, y_b, y_cm, w_out, x, nw)


def _rope_tables(S):
    half = ROT_DIM // 2
    inv = ROPE_THETA ** (-jnp.arange(0, ROT_DIM, 2, dtype=F32) / ROT_DIM)
    ang = jnp.arange(S, dtype=F32)[:, None] * inv[None, :]
    cos, sin = jnp.cos(ang), jnp.sin(ang)
    ones = jnp.ones((S, DQK - ROT_DIM), F32)
    zeros = jnp.zeros((S, DQK - ROT_DIM), F32)
    z8 = jnp.zeros((S, half), F32)
    c_map = jnp.concatenate([cos, cos, ones], axis=1)
    sa_map = jnp.concatenate([z8, sin, zeros], axis=1)
    sb_map = jnp.concatenate([-sin, z8, zeros], axis=1)
    two = lambda t: jnp.concatenate([t, t], axis=1)
    return two(c_map), two(sa_map), two(sb_map)


def _row_info(col, N):
    B = col.shape[0]
    c4 = col.reshape(B, N, GDN_CHUNK, LANES)
    g_st = jnp.swapaxes(c4[..., 0:8], 2, 3).reshape(B, N, 2, N_HEADS * GDN_CHUNK)
    tot = jnp.broadcast_to(c4[:, :, 0, 16:24][..., None], (B, N, 8, N_HEADS * GDN_CHUNK))
    pad = jnp.zeros((B, N, 6, N_HEADS * GDN_CHUNK), F32)
    return jnp.concatenate([g_st, pad, tot], axis=2)


def _prep_layer_weights(l, norm_pre, norm_post, norm_mem, w_in, gdn_conv, gdn_A_log, gdn_dt_bias,
                        gdn_norm, diff_lambda, diff_norm, conv_w, w_mem_kv, w_out):
    w = w_in[l]
    q_end = 3 * W_BR
    n_db = 2 * N_HEADS
    dec = w[:, q_end:q_end + n_db]
    bet = w[:, q_end + n_db:q_end + 2 * n_db]
    w_main = jnp.concatenate([w[:, :q_end], w[:, q_end + 2 * n_db:]], axis=1).astype(BF16)
    w_gb = jnp.concatenate([dec, bet, dec, jnp.zeros((w.shape[0], LANES - 3 * n_db), F32)],
                           axis=1).astype(BF16)
    lane_pad = lambda t: jnp.concatenate(
        [t.reshape(1, n_db), jnp.zeros((1, n_db), F32), t.reshape(1, n_db),
         jnp.zeros((1, LANES - 3 * n_db), F32)], axis=1)
    return dict(
        norm_pre=norm_pre[l][None, :], norm_post=norm_post[l][None, :], norm_mem=norm_mem[l][None, :],
        w_main=w_main, w_gb=w_gb, alog=lane_pad(gdn_A_log[l]), dtb=lane_pad(gdn_dt_bias[l]),
        gdn_conv=gdn_conv[l], gdn_norm=gdn_norm[l][None, :], diff_lambda=diff_lambda[l],
        diff_norm=diff_norm[l][None, :], conv_w=conv_w[l], w_kv=w_mem_kv[l].astype(BF16),
        w_out=w_out[l].astype(BF16))


def _layer(x, mem, p, lambda_init, tables):
    B, S, _ = x.shape
    N = S // GDN_CHUNK
    proj, gb = _in_proj(x, p["norm_pre"], p["w_main"], p["w_gb"], p["alog"], p["dtb"])
    kv = _kv_proj(mem, p["norm_mem"], p["w_kv"])
    qkv, col = _gdn_prep(proj, gb, p["gdn_conv"])
    row = _row_info(col, N)
    o_f, o_b = _gdn(qkv.reshape(B, N, GDN_CHUNK, 3 * W_BR), col.reshape(B, N, GDN_CHUNK, LANES), row)
    y_a = _gdn_post(o_f.reshape(B, S, W_BR), o_b.reshape(B, S, W_BR), proj, p["gdn_norm"])
    q_rot, k_rot, v_t = _rope(proj, *tables)
    y_b = _diff_attn(q_rot, k_rot, v_t, proj, p["diff_lambda"], p["diff_norm"], lambda_init)
    y_cm = _conv_mem(proj, kv, p["conv_w"])
    return _out_proj(y_a, y_b, y_cm, p["w_out"], x, p["norm_post"])


def _trunk(x, mem, layer_params):
    tables = _rope_tables(x.shape[1])
    for l, p in enumerate(layer_params):
        lambda_init = 0.8 - 0.6 * math.exp(-0.3 * l)
        x = _layer(x, mem, p, lambda_init, tables)
    return x


def kernel(x_prompt, x_sample, mem_prompt, mem_sample, norm_pre, norm_post, norm_mem, w_in, gdn_conv,
           gdn_A_log, gdn_dt_bias, gdn_norm, diff_lambda, diff_norm, conv_w, w_mem_kv, w_out):
    depth = w_in.shape[0]
    params = [_prep_layer_weights(l, norm_pre, norm_post, norm_mem, w_in, gdn_conv, gdn_A_log,
                                  gdn_dt_bias, gdn_norm, diff_lambda, diff_norm, conv_w, w_mem_kv, w_out)
              for l in range(depth)]
    return (_trunk(x_prompt, mem_prompt, params), _trunk(x_sample, mem_sample, params))
```

```python
import functools
import math

import jax
import jax.numpy as jnp
from jax import lax
from jax.experimental import pallas as pl
from jax.experimental.pallas import tpu as pltpu

F32 = jnp.float32
BF16 = jnp.bfloat16
EPS = 1e-6

W_BR = 512
N_HEADS = 4
D_HEAD = 128
DQK = 64
ROT_DIM = 16
ROPE_THETA = 500000.0
GDN_CONV = 5
GDN_CHUNK = 64
CONV_W = 3
N_MAIN = 14 * W_BR
LANES = 128
HALO = 16
VT_ROWS = D_HEAD + 16
DIFF_SUB = 256
DIFF_KEYS = 512
NEG = -1e30
VMEM_LIMIT = 56 * 1024 * 1024

COL_A_Z = 3
COL_B_Q, COL_B_K, COL_B_V, COL_B_Z = 4, 5, 6, 7
COL_C_B, COL_C_C, COL_C_X, COL_C_Z = 8, 9, 10, 11
COL_M_Q, COL_M_Z = 12, 13


def _cparams(*sem):
    return pltpu.CompilerParams(dimension_semantics=sem, vmem_limit_bytes=VMEM_LIMIT)


def _sigmoid(x):
    return 1.0 / (1.0 + jnp.exp(-x))


def _silu(x):
    return x * _sigmoid(x)


def _softplus(x):
    return jnp.maximum(x, 0.0) + jnp.log(1.0 + jnp.exp(-jnp.abs(x)))


def _dot(a, b):
    return jnp.dot(a.astype(BF16), b.astype(BF16), preferred_element_type=F32)


def _dot_nt(a, b):
    return lax.dot_general(a.astype(BF16), b.astype(BF16), (((1,), (1,)), ((), ())),
                           preferred_element_type=F32)


def _dot_tn(a, b):
    return lax.dot_general(a.astype(BF16), b.astype(BF16), (((0,), (0,)), ((), ())),
                           preferred_element_type=F32)


def _same_block(r_i, c_i, size):
    shift = size.bit_length() - 1
    return jnp.right_shift(r_i, shift) == jnp.right_shift(c_i, shift)


def _split3(x):
    hi = x.astype(BF16)
    r1 = x - hi.astype(F32)
    mid = r1.astype(BF16)
    lo = (r1 - mid.astype(F32)).astype(BF16)
    return hi, mid, lo


def _in_proj_kernel(x_ref, nw_ref, w_ref, wgb_ref, alog_ref, dtb_ref, o_ref, gb_ref, h_ref):
    @pl.when(pl.program_id(2) == 0)
    def _():
        x = x_ref[...]
        ms = jnp.mean(x * x, axis=-1, keepdims=True)
        h = (x * lax.rsqrt(ms + EPS) * nw_ref[...]).astype(BF16)
        h_ref[...] = h
        raw = jnp.dot(h, wgb_ref[...], preferred_element_type=F32)
        lane = lax.broadcasted_iota(jnp.int32, raw.shape, 1)
        g = -jnp.exp(alog_ref[...]) * _softplus(raw + dtb_ref[...])
        is_beta = (lane >= 8) & (lane < 16)
        gb_ref[...] = jnp.where(is_beta, _sigmoid(raw), g)

    o_ref[...] = jnp.dot(h_ref[...], w_ref[...], preferred_element_type=F32).astype(o_ref.dtype)


def _in_proj(x, nw, w_main, w_gb, alog, dtb):
    B, S, D = x.shape
    tm = min(1024, S)
    tn = 1024
    return pl.pallas_call(
        _in_proj_kernel,
        out_shape=(jax.ShapeDtypeStruct((B, S, N_MAIN), BF16),
                   jax.ShapeDtypeStruct((B, S, LANES), F32)),
        grid=(B, S // tm, N_MAIN // tn),
        in_specs=[
            pl.BlockSpec((None, tm, D), lambda b, i, j: (b, i, 0)),
            pl.BlockSpec((1, D), lambda b, i, j: (0, 0)),
            pl.BlockSpec((D, tn), lambda b, i, j: (0, j)),
            pl.BlockSpec((D, LANES), lambda b, i, j: (0, 0)),
            pl.BlockSpec((1, LANES), lambda b, i, j: (0, 0)),
            pl.BlockSpec((1, LANES), lambda b, i, j: (0, 0)),
        ],
        out_specs=(pl.BlockSpec((None, tm, tn), lambda b, i, j: (b, i, j)),
                   pl.BlockSpec((None, tm, LANES), lambda b, i, j: (b, i, 0))),
        scratch_shapes=[pltpu.VMEM((tm, D), BF16)],
        compiler_params=_cparams("parallel", "parallel", "arbitrary"),
        name="in_proj",
    )(x, nw, w_main, w_gb, alog, dtb)


def _kv_proj_kernel(x_ref, nw_ref, w_ref, o_ref):
    x = x_ref[...]
    ms = jnp.mean(x * x, axis=-1, keepdims=True)
    h = (x * lax.rsqrt(ms + EPS) * nw_ref[...]).astype(BF16)
    o_ref[...] = jnp.dot(h, w_ref[...], preferred_element_type=F32).astype(o_ref.dtype)


def _kv_proj(mem, nw, w_kv):
    B, M, D = mem.shape
    n_out = w_kv.shape[1]
    return pl.pallas_call(
        _kv_proj_kernel,
        out_shape=jax.ShapeDtypeStruct((B, M, n_out), BF16),
        grid=(B,),
        in_specs=[
            pl.BlockSpec((None, M, D), lambda b: (b, 0, 0)),
            pl.BlockSpec((1, D), lambda b: (0, 0)),
            pl.BlockSpec((D, n_out), lambda b: (0, 0)),
        ],
        out_specs=pl.BlockSpec((None, M, n_out), lambda b: (b, 0, 0)),
        compiler_params=_cparams("parallel"),
        name="kv_proj",
    )(mem, nw, w_kv)


def _gdn_prep_kernel(xp_ref, x_ref, xn_ref, gb_ref, cw_ref, qkv_ref, col_ref):
    i = pl.program_id(1)
    n_i = pl.num_programs(1)
    T = x_ref.shape[0]
    pad = GDN_CONV // 2
    prev = jnp.where(i > 0, xp_ref[HALO - 8:, :].astype(F32), 0.0)
    nxt = jnp.where(i < n_i - 1, xn_ref[:8, :].astype(F32), 0.0)
    xe = jnp.concatenate([prev, x_ref[...].astype(F32), nxt], axis=0)
    acc = xe[8 - pad:8 - pad + T, :] * cw_ref[0:1, :]
    for d in range(1, GDN_CONV):
        acc = acc + xe[8 - pad + d:8 - pad + d + T, :] * cw_ref[d:d + 1, :]
    y = _silu(acc)
    for h in range(2 * N_HEADS):
        sl = slice(h * D_HEAD, (h + 1) * D_HEAD)
        t = y[:, sl]
        r = lax.rsqrt(jnp.sum(t * t, axis=-1, keepdims=True) + EPS)
        if h < N_HEADS:
            r = r * (D_HEAD ** -0.5)
        qkv_ref[:, sl] = t * r
    qkv_ref[:, 2 * W_BR:] = y[:, 2 * W_BR:]

    gb = gb_ref[...]
    r_i = lax.broadcasted_iota(jnp.int32, (T, T), 0)
    c_i = lax.broadcasted_iota(jnp.int32, (T, T), 1)
    same = _same_block(r_i, c_i, GDN_CHUNK)
    m_f = jnp.where(same & (c_i <= r_i), 1.0, 0.0).astype(BF16)
    m_b = jnp.where(same & (c_i >= r_i), 1.0, 0.0).astype(BF16)
    m_t = jnp.where(same, 1.0, 0.0).astype(BF16)
    parts = _split3(gb)
    cum_f = sum(jnp.dot(m_f, p, preferred_element_type=F32) for p in parts)
    cum_b = sum(jnp.dot(m_b, p, preferred_element_type=F32) for p in parts)
    tot = sum(jnp.dot(m_t, p, preferred_element_type=F32) for p in parts)
    lane = lax.broadcasted_iota(jnp.int32, gb.shape, 1)
    col_ref[...] = jnp.where(lane < 4, cum_f,
                             jnp.where(lane < 8, cum_b,
                                       jnp.where(lane < 16, gb, tot)))


def _gdn_prep(proj, gb, conv_w):
    B, S, _ = proj.shape
    T = min(512, S)
    nb = T // HALO
    n_halo = S // HALO
    wq = 3 * W_BR
    return pl.pallas_call(
        _gdn_prep_kernel,
        out_shape=(jax.ShapeDtypeStruct((B, S, wq), F32),
                   jax.ShapeDtypeStruct((B, S, LANES), F32)),
        grid=(B, S // T),
        in_specs=[
            pl.BlockSpec((None, HALO, wq), lambda b, i: (b, jnp.maximum(i * nb - 1, 0), 0)),
            pl.BlockSpec((None, T, wq), lambda b, i: (b, i, 0)),
            pl.BlockSpec((None, HALO, wq), lambda b, i: (b, jnp.minimum((i + 1) * nb, n_halo - 1), 0)),
            pl.BlockSpec((None, T, LANES), lambda b, i: (b, i, 0)),
            pl.BlockSpec((GDN_CONV, wq), lambda b, i: (0, 0)),
        ],
        out_specs=(pl.BlockSpec((None, T, wq), lambda b, i: (b, i, 0)),
                   pl.BlockSpec((None, T, LANES), lambda b, i: (b, i, 0))),
        compiler_params=_cparams("parallel", "parallel"),
        name="gdn_prep",
    )(proj, proj, proj, gb, conv_w)


def _gdn_masks(reverse):
    n = N_HEADS * GDN_CHUNK
    r_i = lax.broadcasted_iota(jnp.int32, (n, n), 0)
    c_i = lax.broadcasted_iota(jnp.int32, (n, n), 1)
    same_head = _same_block(r_i, c_i, GDN_CHUNK)
    same_blk = _same_block(r_i, c_i, 8)
    if reverse:
        incl = same_head & (c_i >= r_i)
        strict = same_head & (c_i > r_i)
    else:
        incl = same_head & (c_i <= r_i)
        strict = same_head & (c_i < r_i)
    return incl, strict, same_blk


def _stack_heads(x):
    return jnp.concatenate([x[:, h * D_HEAD:(h + 1) * D_HEAD] for h in range(N_HEADS)], axis=0)


def _mm(a, b):
    return jnp.dot(a, b, preferred_element_type=F32)


def _bf(xs):
    return [x.astype(BF16) for x in xs]


def _unit_tri_solve(a_mats, rhss, same_blks):
    n = a_mats[0].shape[0]
    eye = jnp.where(lax.broadcasted_iota(jnp.int32, (n, n), 0)
                    == lax.broadcasted_iota(jnp.int32, (n, n), 1), 1.0, 0.0)
    d1 = [jnp.where(m, a, 0.0) for a, m in zip(a_mats, same_blks)]
    lo_rhs = _bf([jnp.concatenate([a - d, r], axis=1) for a, d, r in zip(a_mats, d1, rhss)])
    d1b = _bf(d1)
    d2 = [_mm(d, d) for d in d1b]
    d2b = _bf(d2)
    d4b = _bf([_mm(d, d) for d in d2b])
    d3 = [_mm(a, b) for a, b in zip(d1b, d2b)]
    p1 = [eye - a + b - c for a, b, c in zip(d1, d2, d3)]
    t8b = _bf([p + _mm(p.astype(BF16), d) for p, d in zip(p1, d4b)])
    by = [_mm(t, w) for t, w in zip(t8b, lo_rhs)]
    b1b = _bf([x[:, :n] for x in by])
    b2b = _bf([_mm(b, b) for b in b1b])
    b4b = _bf([_mm(b, b) for b in b2b])
    z = [x[:, n:] for x in by]
    for bb in (b1b, b2b, b4b):
        sign = -1.0 if bb is b1b else 1.0
        z = [a + sign * _mm(b, a.astype(BF16)) for a, b in zip(z, bb)]
    return z


def _gdn_prepare(q, k, v, col, row, d):
    g_idx = d * N_HEADS
    b_idx = 8 + d * N_HEADS
    t_idx = 16 + d * N_HEADS
    qs, kbs, ks, rhs_v, rhs_k, gcs, kds, e_tot = [], [], [], [], [], [], [], []
    for h in range(N_HEADS):
        sl = slice(h * D_HEAD, (h + 1) * D_HEAD)
        gc = col[:, g_idx + h:g_idx + h + 1]
        bc = col[:, b_idx + h:b_idx + h + 1]
        gt = col[:, t_idx + h:t_idx + h + 1]
        eg = jnp.exp(gc)
        k_h = k[:, sl]
        kb = k_h * bc
        qs.append(q[:, sl] * eg)
        kbs.append(kb)
        ks.append(k_h)
        rhs_v.append(v[:, sl] * bc)
        rhs_k.append(kb * eg)
        gcs.append(jnp.broadcast_to(gc, (GDN_CHUNK, N_HEADS * GDN_CHUNK)))
        kds.append((k_h * jnp.exp(gt - gc)).astype(BF16))
        e_tot.append(jnp.exp(row[8 + g_idx + h:8 + g_idx + h + 1, :D_HEAD]))
    return dict(
        gram_lhs=jnp.concatenate(kbs + [_stack_heads(q)], axis=0).astype(BF16),
        k_st=jnp.concatenate(ks, axis=0).astype(BF16),
        dlog=jnp.concatenate(gcs, axis=0) - row[d:d + 1, :],
        rhs=jnp.concatenate([jnp.concatenate(rhs_v, axis=0), jnp.concatenate(rhs_k, axis=0)], axis=1),
        qs=jnp.concatenate(qs, axis=0), kds=kds, e_tot=e_tot)


def _gdn_affine_terms(chains, masks):
    n = N_HEADS * GDN_CHUNK
    gram = [lax.dot_general(c["gram_lhs"], c["k_st"], (((1,), (1,)), ((), ())),
                            preferred_element_type=F32) for c in chains]
    decay = [jnp.exp(jnp.where(m[0], c["dlog"], NEG)) for c, m in zip(chains, masks)]
    a_mats = [jnp.where(m[1], g[:n] * dc, 0.0) for g, dc, m in zip(gram, decay, masks)]
    attn_b = _bf([g[n:] * dc for g, dc in zip(gram, decay)])
    xb = _bf(_unit_tri_solve(a_mats, [c["rhs"] for c in chains], [m[2] for m in masks]))
    ax = [_mm(a, x) for a, x in zip(attn_b, xb)]
    out = []
    for c, x, a in zip(chains, xb, ax):
        heads = []
        for h in range(N_HEADS):
            rs = slice(h * GDN_CHUNK, (h + 1) * GDN_CHUNK)
            kx = lax.dot_general(c["kds"][h], x[rs], (((0,), (0,)), ((), ())),
                                 preferred_element_type=F32)
            p = c["qs"][rs] - a[rs, D_HEAD:]
            heads.append(dict(lhs=jnp.concatenate([kx[:, D_HEAD:], p], axis=0).astype(BF16),
                              n=kx[:, :D_HEAD], r=a[rs, :D_HEAD], e=c["e_tot"][h]))
        out.append(heads)
    return out


def _gdn_kernel(qf_ref, kf_ref, vf_ref, colf_ref, rowf_ref,
                qb_ref, kb_ref, vb_ref, colb_ref, rowb_ref,
                of_ref, ob_ref, s_ref):
    nc = qf_ref.shape[0]

    @pl.when(pl.program_id(1) == 0)
    def _():
        s_ref[...] = jnp.zeros(s_ref.shape, F32)

    masks_f = _gdn_masks(False)
    masks_b = _gdn_masks(True)
    chains, masks = [], []
    for c in range(nc):
        cb = nc - 1 - c
        chains.append(_gdn_prepare(qf_ref[c], kf_ref[c], vf_ref[c], colf_ref[c], rowf_ref[c], 0))
        chains.append(_gdn_prepare(qb_ref[cb], kb_ref[cb], vb_ref[cb], colb_ref[cb], rowb_ref[cb], 1))
        masks += [masks_f, masks_b]
    terms = _gdn_affine_terms(chains, masks)

    state = [[s_ref[d, h] for h in range(N_HEADS)] for d in range(2)]
    for c in range(nc):
        ls = [[_mm(terms[2 * c + d][h]["lhs"], state[d][h].astype(BF16)) for h in range(N_HEADS)]
              for d in range(2)]
        for d, o_ref, idx in ((0, of_ref, c), (1, ob_ref, nc - 1 - c)):
            outs = []
            for h in range(N_HEADS):
                t = terms[2 * c + d][h]
                state[d][h] = state[d][h] * t["e"] - ls[d][h][:D_HEAD] + t["n"]
                outs.append(ls[d][h][D_HEAD:] + t["r"])
            o_ref[idx] = jnp.concatenate(outs, axis=1)
    for d in range(2):
        for h in range(N_HEADS):
            s_ref[d, h] = state[d][h]


def _gdn(qkv, col, row):
    B, N, C, _ = qkv.shape
    nc = min(4, N)
    nblk = N // nc
    fwd = lambda j: (lambda b, i: (b, i, 0, j))
    bwd = lambda j: (lambda b, i: (b, nblk - 1 - i, 0, j))
    specs = []
    for mk in (fwd, bwd):
        specs += [pl.BlockSpec((None, nc, C, W_BR), mk(0)),
                  pl.BlockSpec((None, nc, C, W_BR), mk(1)),
                  pl.BlockSpec((None, nc, C, W_BR), mk(2)),
                  pl.BlockSpec((None, nc, C, LANES), mk(0)),
                  pl.BlockSpec((None, nc, 16, 2 * LANES), mk(0))]
    out_sds = jax.ShapeDtypeStruct((B, N, C, W_BR), F32)
    return pl.pallas_call(
        _gdn_kernel,
        out_shape=(out_sds, out_sds),
        grid=(B, nblk),
        in_specs=specs,
        out_specs=(pl.BlockSpec((None, nc, C, W_BR), fwd(0)),
                   pl.BlockSpec((None, nc, C, W_BR), bwd(0))),
        scratch_shapes=[pltpu.VMEM((2, N_HEADS, D_HEAD, D_HEAD), F32)],
        compiler_params=_cparams("parallel", "arbitrary"),
        name="gdn",
    )(qkv, qkv, qkv, col, row, qkv, qkv, qkv, col, row)


def _gdn_post_kernel(of_ref, ob_ref, z_ref, nw_ref, y_ref):
    o = of_ref[...] + ob_ref[...]
    z = z_ref[...].astype(F32)
    for h in range(N_HEADS):
        sl = slice(h * D_HEAD, (h + 1) * D_HEAD)
        t = o[:, sl]
        r = lax.rsqrt(jnp.mean(t * t, axis=-1, keepdims=True) + EPS)
        y_ref[:, sl] = (t * r * nw_ref[...] * _silu(z[:, sl])).astype(y_ref.dtype)


def _gdn_post(o_f, o_b, proj, nw):
    B, S, _ = o_f.shape
    T = min(1024, S)
    return pl.pallas_call(
        _gdn_post_kernel,
        out_shape=jax.ShapeDtypeStruct((B, S, W_BR), BF16),
        grid=(B, S // T),
        in_specs=[
            pl.BlockSpec((None, T, W_BR), lambda b, i: (b, i, 0)),
            pl.BlockSpec((None, T, W_BR), lambda b, i: (b, i, 0)),
            pl.BlockSpec((None, T, W_BR), lambda b, i: (b, i, COL_A_Z)),
            pl.BlockSpec((1, D_HEAD), lambda b, i: (0, 0)),
        ],
        out_specs=pl.BlockSpec((None, T, W_BR), lambda b, i: (b, i, 0)),
        compiler_params=_cparams("parallel", "parallel"),
        name="gdn_post",
    )(o_f, o_b, proj, nw)


def _rope_kernel(q_ref, k_ref, v_ref, c_ref, sa_ref, sb_ref, qo_ref, ko_ref, vt_ref):
    c = c_ref[...]
    sa = sa_ref[...]
    sb = sb_ref[...]
    half = ROT_DIM // 2
    q_scale = DQK ** -0.5 * math.log2(math.e)
    for src, dst, scale in ((q_ref, qo_ref, q_scale), (k_ref, ko_ref, 1.0)):
        for h in range(N_HEADS):
            sl = slice(h * D_HEAD, (h + 1) * D_HEAD)
            x = src[:, sl].astype(F32)
            y = (x * c + pltpu.roll(x, half, 1) * sa + pltpu.roll(x, D_HEAD - half, 1) * sb)
            dst[:, sl] = (y * scale).astype(dst.dtype)
    v_t = v_ref[...].astype(F32).T.astype(vt_ref.dtype)
    ones = jnp.ones((VT_ROWS - D_HEAD, v_t.shape[1]), vt_ref.dtype)
    for h in range(N_HEADS):
        vt_ref[h, :D_HEAD, :] = v_t[h * D_HEAD:(h + 1) * D_HEAD, :]
        vt_ref[h, D_HEAD:, :] = ones


def _rope(proj, cos_t, sin_a, sin_b):
    B, S, _ = proj.shape
    T = min(1024, S)
    sds = jax.ShapeDtypeStruct((B, S, W_BR), BF16)
    tab = pl.BlockSpec((T, D_HEAD), lambda b, i: (i, 0))
    return pl.pallas_call(
        _rope_kernel,
        out_shape=(sds, sds, jax.ShapeDtypeStruct((B, N_HEADS, VT_ROWS, S), BF16)),
        grid=(B, S // T),
        in_specs=[
            pl.BlockSpec((None, T, W_BR), lambda b, i: (b, i, COL_B_Q)),
            pl.BlockSpec((None, T, W_BR), lambda b, i: (b, i, COL_B_K)),
            pl.BlockSpec((None, T, W_BR), lambda b, i: (b, i, COL_B_V)),
            tab, tab, tab,
        ],
        out_specs=(pl.BlockSpec((None, T, W_BR), lambda b, i: (b, i, 0)),
                   pl.BlockSpec((None, T, W_BR), lambda b, i: (b, i, 0)),
                   pl.BlockSpec((None, N_HEADS, VT_ROWS, T), lambda b, i: (b, 0, 0, i))),
        compiler_params=_cparams("parallel", "parallel"),
        name="rope",
    )(proj, proj, proj, cos_t, sin_a, sin_b)


def _diff_attn_kernel(q_ref, k_ref, vt_ref, z_ref, lp_ref, nw_ref, y_ref, *, lambda_init):
    tq = q_ref.shape[0]
    sub = min(DIFF_SUB, tq)
    n_keys = k_ref.shape[0]
    ck = min(DIFF_KEYS, n_keys)
    lane = lax.broadcasted_iota(jnp.int32, (sub, D_HEAD), 1)
    lp = lp_ref[...]
    lam = (jnp.exp(jnp.sum(lp[0:1] * lp[1:2], axis=-1, keepdims=True))
           - jnp.exp(jnp.sum(lp[2:3] * lp[3:4], axis=-1, keepdims=True)) + lambda_init)
    streams = [(j, first) for j in range(tq // sub) for first in (True, False)]
    q_masked = []
    for j, first in streams:
        q = q_ref[j * sub:(j + 1) * sub, :]
        q_masked.append(jnp.where((lane < DQK) == first, q, jnp.zeros_like(q)))

    def scores(c, t):
        k_c = k_ref[c * ck:(c + 1) * ck, :]
        return lax.dot_general(k_c, q_masked[t], (((1,), (1,)), ((), ())),
                               preferred_element_type=F32)

    run_max = [None] * len(streams)
    acc = [None] * len(streams)
    pending = [scores(0, t) for t in range(len(streams))]
    for c in range(n_keys // ck):
        vt_c = vt_ref[:, c * ck:(c + 1) * ck]
        for t in range(len(streams)):
            s = pending[t]
            if (c + 1) * ck < n_keys:
                pending[t] = scores(c + 1, t)
            m_c = jnp.max(s, axis=0, keepdims=True)
            m_new = m_c if c == 0 else jnp.maximum(run_max[t], m_c)
            p = jnp.exp2(s - m_new).astype(BF16)
            pv = jnp.dot(vt_c, p, preferred_element_type=F32)
            acc[t] = pv if c == 0 else acc[t] * jnp.exp2(run_max[t] - m_new) + pv
            run_max[t] = m_new

    for j in range(tq // sub):
        halves = []
        for first in (True, False):
            o_aug = acc[streams.index((j, first))]
            inv = 1.0 / o_aug[D_HEAD:D_HEAD + 1, :]
            halves.append(o_aug[:D_HEAD, :] * (inv if first else inv * lam))
        o = (halves[0] - halves[1]).T
        r = lax.rsqrt(jnp.mean(o * o, axis=-1, keepdims=True) + EPS)
        o = o * r * nw_ref[...] * (1.0 - lambda_init)
        rows = slice(j * sub, (j + 1) * sub)
        y_ref[rows, :] = (o * _silu(z_ref[rows, :].astype(F32))).astype(y_ref.dtype)


def _diff_attn(q_rot, k_rot, v_t, proj, lam_p, nw, lambda_init):
    B, S, _ = q_rot.shape
    tq = min(4 * DIFF_SUB, S)
    nh = W_BR // D_HEAD
    return pl.pallas_call(
        functools.partial(_diff_attn_kernel, lambda_init=lambda_init),
        out_shape=jax.ShapeDtypeStruct((B, S, W_BR), BF16),
        grid=(B, N_HEADS, S // tq),
        in_specs=[
            pl.BlockSpec((None, tq, D_HEAD), lambda b, h, i: (b, i, h)),
            pl.BlockSpec((None, S, D_HEAD), lambda b, h, i: (b, 0, h)),
            pl.BlockSpec((None, None, VT_ROWS, S), lambda b, h, i: (b, h, 0, 0)),
            pl.BlockSpec((None, tq, D_HEAD), lambda b, h, i: (b, i, COL_B_Z * nh + h)),
            pl.BlockSpec((4, DQK), lambda b, h, i: (0, 0)),
            pl.BlockSpec((1, D_HEAD), lambda b, h, i: (0, 0)),
        ],
        out_specs=pl.BlockSpec((None, tq, D_HEAD), lambda b, h, i: (b, i, h)),
        compiler_params=_cparams("parallel", "parallel", "parallel"),
        name="diff_attn",
    )(q_rot, k_rot, v_t, proj, lam_p, nw)


def _conv_mem_kernel(cb_ref, cc_ref, cx_ref, cz_ref, ccp_ref, cxp_ref, ccn_ref, cxn_ref,
                     mq_ref, mz_ref, kv_ref, cw_ref, y_ref):
    i = pl.program_id(1)
    n_i = pl.num_programs(1)
    T = cb_ref.shape[0]
    prev = jnp.where(i > 0, ccp_ref[HALO - 8:, :].astype(F32) * cxp_ref[HALO - 8:, :].astype(F32), 0.0)
    nxt = jnp.where(i < n_i - 1, ccn_ref[:8, :].astype(F32) * cxn_ref[:8, :].astype(F32), 0.0)
    cur = cc_ref[...].astype(F32) * cx_ref[...].astype(F32)
    xe = jnp.concatenate([prev, cur, nxt], axis=0)
    pad = CONV_W // 2
    acc = xe[8 - pad:8 - pad + T, :] * cw_ref[0:1, :]
    for d in range(1, CONV_W):
        acc = acc + xe[8 - pad + d:8 - pad + d + T, :] * cw_ref[d:d + 1, :]
    y_c = cb_ref[...].astype(F32) * acc * _silu(cz_ref[...].astype(F32))
    y_ref[:, :W_BR] = y_c.astype(y_ref.dtype)

    mz = mz_ref[...].astype(F32)
    for h in range(N_HEADS):
        sl = slice(h * D_HEAD, (h + 1) * D_HEAD)
        s = _dot_nt(mq_ref[:, sl], kv_ref[:, sl]) * (D_HEAD ** -0.5)
        p = jnp.exp(s - jnp.max(s, axis=-1, keepdims=True))
        p = (p * (1.0 / jnp.sum(p, axis=-1, keepdims=True))).astype(BF16)
        o = jnp.dot(p, kv_ref[:, W_BR + h * D_HEAD:W_BR + (h + 1) * D_HEAD],
                    preferred_element_type=F32)
        y_ref[:, W_BR + h * D_HEAD:W_BR + (h + 1) * D_HEAD] = (o * _silu(mz[:, sl])).astype(y_ref.dtype)


def _conv_mem(proj, kv, conv_w):
    B, S, _ = proj.shape
    M = kv.shape[1]
    T = min(512, S)
    nb = T // HALO
    n_halo = S // HALO
    main = lambda c: pl.BlockSpec((None, T, W_BR), lambda b, i: (b, i, c))
    prev = lambda c: pl.BlockSpec((None, HALO, W_BR), lambda b, i: (b, jnp.maximum(i * nb - 1, 0), c))
    nxt = lambda c: pl.BlockSpec((None, HALO, W_BR),
                                 lambda b, i: (b, jnp.minimum((i + 1) * nb, n_halo - 1), c))
    return pl.pallas_call(
        _conv_mem_kernel,
        out_shape=jax.ShapeDtypeStruct((B, S, 2 * W_BR), BF16),
        grid=(B, S // T),
        in_specs=[main(COL_C_B), main(COL_C_C), main(COL_C_X), main(COL_C_Z),
                  prev(COL_C_C), prev(COL_C_X), nxt(COL_C_C), nxt(COL_C_X),
                  main(COL_M_Q), main(COL_M_Z),
                  pl.BlockSpec((None, M, 2 * W_BR), lambda b, i: (b, 0, 0)),
                  pl.BlockSpec((CONV_W, W_BR), lambda b, i: (0, 0))],
        out_specs=pl.BlockSpec((None, T, 2 * W_BR), lambda b, i: (b, i, 0)),
        compiler_params=_cparams("parallel", "parallel"),
        name="conv_mem",
    )(proj, proj, proj, proj, proj, proj, proj, proj, proj, proj, kv, conv_w)


def _out_proj_kernel(ya_ref, yb_ref, ycm_ref, w_ref, x_ref, nw_ref, o_ref):
    y = jnp.dot(ya_ref[...], w_ref[0:W_BR, :], preferred_element_type=F32)
    y = y + jnp.dot(yb_ref[...], w_ref[W_BR:2 * W_BR, :], preferred_element_type=F32)
    y = y + jnp.dot(ycm_ref[...], w_ref[2 * W_BR:, :], preferred_element_type=F32)
    r = lax.rsqrt(jnp.mean(y * y, axis=-1, keepdims=True) + EPS)
    o_ref[...] = x_ref[...] + y * r * nw_ref[...]


def _out_proj(y_a, y_b, y_cm, w_out, x, nw):
    B, S, D = x.shape
    T = min(512, S)
    return pl.pallas_call(
        _out_proj_kernel,
        out_shape=jax.ShapeDtypeStruct((B, S, D), F32),
        grid=(B, S // T),
        in_specs=[
            pl.BlockSpec((None, T, W_BR), lambda b, i: (b, i, 0)),
            pl.BlockSpec((None, T, W_BR), lambda b, i: (b, i, 0)),
            pl.BlockSpec((None, T, 2 * W_BR), lambda b, i: (b, i, 0)),
            pl.BlockSpec((4 * W_BR, D), lambda b, i: (0, 0)),
            pl.BlockSpec((None, T, D), lambda b, i: (b, i, 0)),
            pl.BlockSpec((1, D), lambda b, i: (0, 0)),
        ],
        out_specs=pl.BlockSpec((None, T, D), lambda b, i: (b, i, 0)),
        compiler_params=_cparams("parallel", "parallel"),
        name="out_proj",
    )(y_a, y_b, y_cm, w_out, x, nw)


def _rope_tables(S):
    half = ROT_DIM // 2
    inv = ROPE_THETA ** (-jnp.arange(0, ROT_DIM, 2, dtype=F32) / ROT_DIM)
    ang = jnp.arange(S, dtype=F32)[:, None] * inv[None, :]
    cos, sin = jnp.cos(ang), jnp.sin(ang)
    ones = jnp.ones((S, DQK - ROT_DIM), F32)
    zeros = jnp.zeros((S, DQK - ROT_DIM), F32)
    z8 = jnp.zeros((S, half), F32)
    c_map = jnp.concatenate([cos, cos, ones], axis=1)
    sa_map = jnp.concatenate([z8, sin, zeros], axis=1)
    sb_map = jnp.concatenate([-sin, z8, zeros], axis=1)
    two = lambda t: jnp.concatenate([t, t], axis=1)
    return two(c_map), two(sa_map), two(sb_map)


def _row_info(col, N):
    B = col.shape[0]
    c4 = col.reshape(B, N, GDN_CHUNK, LANES)
    g_st = jnp.swapaxes(c4[..., 0:8], 2, 3).reshape(B, N, 2, N_HEADS * GDN_CHUNK)
    tot = jnp.broadcast_to(c4[:, :, 0, 16:24][..., None], (B, N, 8, N_HEADS * GDN_CHUNK))
    pad = jnp.zeros((B, N, 6, N_HEADS * GDN_CHUNK), F32)
    return jnp.concatenate([g_st, pad, tot], axis=2)


def _prep_layer_weights(l, norm_pre, norm_post, norm_mem, w_in, gdn_conv, gdn_A_log, gdn_dt_bias,
                        gdn_norm, diff_lambda, diff_norm, conv_w, w_mem_kv, w_out):
    w = w_in[l]
    q_end = 3 * W_BR
    n_db = 2 * N_HEADS
    dec = w[:, q_end:q_end + n_db]
    bet = w[:, q_end + n_db:q_end + 2 * n_db]
    w_main = jnp.concatenate([w[:, :q_end], w[:, q_end + 2 * n_db:]], axis=1).astype(BF16)
    w_gb = jnp.concatenate([dec, bet, dec, jnp.zeros((w.shape[0], LANES - 3 * n_db), F32)],
                           axis=1).astype(BF16)
    lane_pad = lambda t: jnp.concatenate(
        [t.reshape(1, n_db), jnp.zeros((1, n_db), F32), t.reshape(1, n_db),
         jnp.zeros((1, LANES - 3 * n_db), F32)], axis=1)
    return dict(
        norm_pre=norm_pre[l][None, :], norm_post=norm_post[l][None, :], norm_mem=norm_mem[l][None, :],
        w_main=w_main, w_gb=w_gb, alog=lane_pad(gdn_A_log[l]), dtb=lane_pad(gdn_dt_bias[l]),
        gdn_conv=gdn_conv[l], gdn_norm=gdn_norm[l][None, :], diff_lambda=diff_lambda[l],
        diff_norm=diff_norm[l][None, :], conv_w=conv_w[l], w_kv=w_mem_kv[l].astype(BF16),
        w_out=w_out[l].astype(BF16))


def _layer(x, mem, p, lambda_init, tables):
    B, S, _ = x.shape
    N = S // GDN_CHUNK
    proj, gb = _in_proj(x, p["norm_pre"], p["w_main"], p["w_gb"], p["alog"], p["dtb"])
    kv = _kv_proj(mem, p["norm_mem"], p["w_kv"])
    qkv, col = _gdn_prep(proj, gb, p["gdn_conv"])
    row = _row_info(col, N)
    o_f, o_b = _gdn(qkv.reshape(B, N, GDN_CHUNK, 3 * W_BR), col.reshape(B, N, GDN_CHUNK, LANES), row)
    y_a = _gdn_post(o_f.reshape(B, S, W_BR), o_b.reshape(B, S, W_BR), proj, p["gdn_norm"])
    q_rot, k_rot, v_t = _rope(proj, *tables)
    y_b = _diff_attn(q_rot, k_rot, v_t, proj, p["diff_lambda"], p["diff_norm"], lambda_init)
    y_cm = _conv_mem(proj, kv, p["conv_w"])
    return _out_proj(y_a, y_b, y_cm, p["w_out"], x, p["norm_post"])


def _trunk(x, mem, layer_params):
    tables = _rope_tables(x.shape[1])
    for l, p in enumerate(layer_params):
        lambda_init = 0.8 - 0.6 * math.exp(-0.3 * l)
        x = _layer(x, mem, p, lambda_init, tables)
    return x


def kernel(x_prompt, x_sample, mem_prompt, mem_sample, norm_pre, norm_post, norm_mem, w_in, gdn_conv,
           gdn_A_log, gdn_dt_bias, gdn_norm, diff_lambda, diff_norm, conv_w, w_mem_kv, w_out):
    depth = w_in.shape[0]
    params = [_prep_layer_weights(l, norm_pre, norm_post, norm_mem, w_in, gdn_conv, gdn_A_log,
                                  gdn_dt_bias, gdn_norm, diff_lambda, diff_norm, conv_w, w_mem_kv, w_out)
              for l in range(depth)]
    return (_trunk(x_prompt, mem_prompt, params), _trunk(x_sample, mem_sample, params))
```

```python
import functools
import math

import jax
import jax.numpy as jnp
from jax import lax
from jax.experimental import pallas as pl
from jax.experimental.pallas import tpu as pltpu

F32 = jnp.float32
BF16 = jnp.bfloat16
EPS = 1e-6

W_BR = 512
N_HEADS = 4
D_HEAD = 128
DQK = 64
ROT_DIM = 16
ROPE_THETA = 500000.0
GDN_CONV = 5
GDN_CHUNK = 64
CONV_W = 3
N_MAIN = 14 * W_BR
LANES = 128
HALO = 16
VT_ROWS = D_HEAD + 16
DIFF_SUB = 256
DIFF_KEYS = 512
NEG = -1e30
VMEM_LIMIT = 56 * 1024 * 1024

COL_A_Z = 3
COL_B_Q, COL_B_K, COL_B_V, COL_B_Z = 4, 5, 6, 7
COL_C_B, COL_C_C, COL_C_X, COL_C_Z = 8, 9, 10, 11
COL_M_Q, COL_M_Z = 12, 13
IN_TILE_COLS = 2 * W_BR
IN_NORM_ROWS = 256
IN_TILE_QK = COL_B_Q // 2
IN_TILE_VZ = COL_B_V // 2


def _cparams(*sem):
    return pltpu.CompilerParams(dimension_semantics=sem, vmem_limit_bytes=VMEM_LIMIT)


def _sigmoid(x):
    return 1.0 / (1.0 + jnp.exp(-x))


def _silu(x):
    return x * _sigmoid(x)


def _softplus(x):
    return jnp.maximum(x, 0.0) + jnp.log(1.0 + jnp.exp(-jnp.abs(x)))


def _dot(a, b):
    return jnp.dot(a.astype(BF16), b.astype(BF16), preferred_element_type=F32)


def _dot_nt(a, b):
    return lax.dot_general(a.astype(BF16), b.astype(BF16), (((1,), (1,)), ((), ())),
                           preferred_element_type=F32)


def _dot_tn(a, b):
    return lax.dot_general(a.astype(BF16), b.astype(BF16), (((0,), (0,)), ((), ())),
                           preferred_element_type=F32)


def _same_block(r_i, c_i, size):
    shift = size.bit_length() - 1
    return jnp.right_shift(r_i, shift) == jnp.right_shift(c_i, shift)


def _split3(x):
    hi = x.astype(BF16)
    r1 = x - hi.astype(F32)
    mid = r1.astype(BF16)
    lo = (r1 - mid.astype(F32)).astype(BF16)
    return hi, mid, lo


def _in_proj_kernel(x_ref, nw_ref, w_ref, wgb_ref, alog_ref, dtb_ref, c_ref, sa_ref, sb_ref,
                    o_ref, gb_ref, vt_ref, h_ref):
    j = pl.program_id(2)

    @pl.when(j == 0)
    def _():
        n_rows = x_ref.shape[0]
        blk = min(IN_NORM_ROWS, n_rows)
        for r0 in range(0, n_rows, blk):
            rows = slice(r0, r0 + blk)
            x = x_ref[rows, :]
            ms = jnp.mean(x * x, axis=-1, keepdims=True)
            h = (x * lax.rsqrt(ms + EPS) * nw_ref[...]).astype(BF16)
            h_ref[rows, :] = h
            raw = jnp.dot(h, wgb_ref[...], preferred_element_type=F32)
            lane = lax.broadcasted_iota(jnp.int32, raw.shape, 1)
            g = -jnp.exp(alog_ref[...]) * _softplus(raw + dtb_ref[...])
            is_beta = (lane >= 8) & (lane < 16)
            gb_ref[rows, :] = jnp.where(is_beta, _sigmoid(raw), g)
            o_ref[rows, :] = jnp.dot(h, w_ref[...], preferred_element_type=F32).astype(o_ref.dtype)

    def project():
        return jnp.dot(h_ref[...], w_ref[...], preferred_element_type=F32)

    @pl.when(j == IN_TILE_QK)
    def _():
        acc = project()
        c = c_ref[...]
        sa = sa_ref[...]
        sb = sb_ref[...]
        half = ROT_DIM // 2
        q_scale = DQK ** -0.5 * math.log2(math.e)
        for hh in range(2 * N_HEADS):
            sl = slice(hh * D_HEAD, (hh + 1) * D_HEAD)
            t = acc[:, sl]
            y = t * c + pltpu.roll(t, half, 1) * sa + pltpu.roll(t, D_HEAD - half, 1) * sb
            if hh < N_HEADS:
                y = y * q_scale
            o_ref[:, sl] = y.astype(o_ref.dtype)

    @pl.when(j == IN_TILE_VZ)
    def _():
        acc = project()
        o_ref[...] = acc.astype(o_ref.dtype)
        ones = jnp.ones((VT_ROWS - D_HEAD, acc.shape[0]), vt_ref.dtype)
        for hh in range(N_HEADS):
            vt_ref[hh, :D_HEAD, :] = acc[:, hh * D_HEAD:(hh + 1) * D_HEAD].T.astype(vt_ref.dtype)
            vt_ref[hh, D_HEAD:, :] = ones

    @pl.when((j != 0) & (j != IN_TILE_QK) & (j != IN_TILE_VZ))
    def _():
        o_ref[...] = project().astype(o_ref.dtype)


def _in_proj(x, nw, w_main, w_gb, alog, dtb, cos_t, sin_a, sin_b):
    B, S, D = x.shape
    tm = min(1024, S)
    tn = IN_TILE_COLS
    tab = pl.BlockSpec((tm, D_HEAD), lambda b, i, j: (i, 0))
    return pl.pallas_call(
        _in_proj_kernel,
        out_shape=(jax.ShapeDtypeStruct((B, S, N_MAIN), BF16),
                   jax.ShapeDtypeStruct((B, S, LANES), F32),
                   jax.ShapeDtypeStruct((B, N_HEADS, VT_ROWS, S), BF16)),
        grid=(B, S // tm, N_MAIN // tn),
        in_specs=[
            pl.BlockSpec((None, tm, D), lambda b, i, j: (b, i, 0)),
            pl.BlockSpec((1, D), lambda b, i, j: (0, 0)),
            pl.BlockSpec((D, tn), lambda b, i, j: (0, j)),
            pl.BlockSpec((D, LANES), lambda b, i, j: (0, 0)),
            pl.BlockSpec((1, LANES), lambda b, i, j: (0, 0)),
            pl.BlockSpec((1, LANES), lambda b, i, j: (0, 0)),
            tab, tab, tab,
        ],
        out_specs=(pl.BlockSpec((None, tm, tn), lambda b, i, j: (b, i, j)),
                   pl.BlockSpec((None, tm, LANES), lambda b, i, j: (b, i, 0)),
                   pl.BlockSpec((None, N_HEADS, VT_ROWS, tm), lambda b, i, j: (b, 0, 0, i))),
        scratch_shapes=[pltpu.VMEM((tm, D), BF16)],
        compiler_params=_cparams("parallel", "parallel", "arbitrary"),
        name="in_proj",
    )(x, nw, w_main, w_gb, alog, dtb, cos_t, sin_a, sin_b)


def _kv_proj_kernel(x_ref, nw_ref, w_ref, o_ref):
    x = x_ref[...]
    ms = jnp.mean(x * x, axis=-1, keepdims=True)
    h = (x * lax.rsqrt(ms + EPS) * nw_ref[...]).astype(BF16)
    o_ref[...] = jnp.dot(h, w_ref[...], preferred_element_type=F32).astype(o_ref.dtype)


def _kv_proj(mem, nw, w_kv):
    B, M, D = mem.shape
    n_out = w_kv.shape[1]
    return pl.pallas_call(
        _kv_proj_kernel,
        out_shape=jax.ShapeDtypeStruct((B, M, n_out), BF16),
        grid=(B,),
        in_specs=[
            pl.BlockSpec((None, M, D), lambda b: (b, 0, 0)),
            pl.BlockSpec((1, D), lambda b: (0, 0)),
            pl.BlockSpec((D, n_out), lambda b: (0, 0)),
        ],
        out_specs=pl.BlockSpec((None, M, n_out), lambda b: (b, 0, 0)),
        compiler_params=_cparams("parallel"),
        name="kv_proj",
    )(mem, nw, w_kv)


def _gdn_prep_kernel(xp_ref, x_ref, xn_ref, gb_ref, cw_ref, cm_ref, qkv_ref, col_ref):
    i = pl.program_id(1)
    n_i = pl.num_programs(1)
    T = x_ref.shape[0]
    pad = GDN_CONV // 2
    prev = jnp.where(i > 0, xp_ref[HALO - 8:, :].astype(F32), 0.0)
    nxt = jnp.where(i < n_i - 1, xn_ref[:8, :].astype(F32), 0.0)
    xe = jnp.concatenate([prev, x_ref[...].astype(F32), nxt], axis=0)
    acc = xe[8 - pad:8 - pad + T, :] * cw_ref[0:1, :]
    for d in range(1, GDN_CONV):
        acc = acc + xe[8 - pad + d:8 - pad + d + T, :] * cw_ref[d:d + 1, :]
    y = _silu(acc)
    for h in range(2 * N_HEADS):
        sl = slice(h * D_HEAD, (h + 1) * D_HEAD)
        t = y[:, sl]
        r = lax.rsqrt(jnp.sum(t * t, axis=-1, keepdims=True) + EPS)
        if h < N_HEADS:
            r = r * (D_HEAD ** -0.5)
        qkv_ref[:, sl] = t * r
    qkv_ref[:, 2 * W_BR:] = y[:, 2 * W_BR:]

    gb = gb_ref[...]
    sums = sum(jnp.dot(cm_ref[...], p, preferred_element_type=F32) for p in _split3(gb))
    lane = lax.broadcasted_iota(jnp.int32, gb.shape, 1)
    col_ref[...] = jnp.where(lane < 4, sums[:T],
                             jnp.where(lane < 8, sums[T:2 * T],
                                       jnp.where(lane < 16, gb, sums[2 * T:])))


def _chunk_sum_matrices(T):
    r_i = lax.broadcasted_iota(jnp.int32, (T, T), 0)
    c_i = lax.broadcasted_iota(jnp.int32, (T, T), 1)
    same = _same_block(r_i, c_i, GDN_CHUNK)
    return jnp.concatenate([same & (c_i <= r_i), same & (c_i >= r_i), same], axis=0).astype(BF16)


def _gdn_prep(proj, gb, conv_w):
    B, S, _ = proj.shape
    T = min(512, S)
    nb = T // HALO
    n_halo = S // HALO
    wq = 3 * W_BR
    return pl.pallas_call(
        _gdn_prep_kernel,
        out_shape=(jax.ShapeDtypeStruct((B, S, wq), F32),
                   jax.ShapeDtypeStruct((B, S, LANES), F32)),
        grid=(B, S // T),
        in_specs=[
            pl.BlockSpec((None, HALO, wq), lambda b, i: (b, jnp.maximum(i * nb - 1, 0), 0)),
            pl.BlockSpec((None, T, wq), lambda b, i: (b, i, 0)),
            pl.BlockSpec((None, HALO, wq), lambda b, i: (b, jnp.minimum((i + 1) * nb, n_halo - 1), 0)),
            pl.BlockSpec((None, T, LANES), lambda b, i: (b, i, 0)),
            pl.BlockSpec((GDN_CONV, wq), lambda b, i: (0, 0)),
            pl.BlockSpec((3 * T, T), lambda b, i: (0, 0)),
        ],
        out_specs=(pl.BlockSpec((None, T, wq), lambda b, i: (b, i, 0)),
                   pl.BlockSpec((None, T, LANES), lambda b, i: (b, i, 0))),
        compiler_params=_cparams("parallel", "parallel"),
        name="gdn_prep",
    )(proj, proj, proj, gb, conv_w, _chunk_sum_matrices(T))


def _gdn_masks(reverse):
    n = N_HEADS * GDN_CHUNK
    r_i = lax.broadcasted_iota(jnp.int32, (n, n), 0)
    c_i = lax.broadcasted_iota(jnp.int32, (n, n), 1)
    same_head = _same_block(r_i, c_i, GDN_CHUNK)
    same_blk = _same_block(r_i, c_i, 8)
    if reverse:
        incl = same_head & (c_i >= r_i)
        strict = same_head & (c_i > r_i)
    else:
        incl = same_head & (c_i <= r_i)
        strict = same_head & (c_i < r_i)
    return incl, strict, same_blk


def _stack_heads(x):
    return jnp.concatenate([x[:, h * D_HEAD:(h + 1) * D_HEAD] for h in range(N_HEADS)], axis=0)


def _mm(a, b):
    return jnp.dot(a, b, preferred_element_type=F32)


def _bf(xs):
    return [x.astype(BF16) for x in xs]


def _unit_tri_solve(a_mats, rhss, same_blks):
    n = a_mats[0].shape[0]
    eye = jnp.where(lax.broadcasted_iota(jnp.int32, (n, n), 0)
                    == lax.broadcasted_iota(jnp.int32, (n, n), 1), 1.0, 0.0)
    d1 = [jnp.where(m, a, 0.0) for a, m in zip(a_mats, same_blks)]
    lo_rhs = _bf([jnp.concatenate([a - d, r], axis=1) for a, d, r in zip(a_mats, d1, rhss)])
    d1b = _bf(d1)
    d2 = [_mm(d, d) for d in d1b]
    d2b = _bf(d2)
    d4b = _bf([_mm(d, d) for d in d2b])
    d3 = [_mm(a, b) for a, b in zip(d1b, d2b)]
    p1 = [eye - a + b - c for a, b, c in zip(d1, d2, d3)]
    t8b = _bf([p + _mm(p.astype(BF16), d) for p, d in zip(p1, d4b)])
    by = [_mm(t, w) for t, w in zip(t8b, lo_rhs)]
    b1b = _bf([x[:, :n] for x in by])
    b2b = _bf([_mm(b, b) for b in b1b])
    b4b = _bf([_mm(b, b) for b in b2b])
    z = [x[:, n:] for x in by]
    for bb in (b1b, b2b, b4b):
        sign = -1.0 if bb is b1b else 1.0
        z = [a + sign * _mm(b, a.astype(BF16)) for a, b in zip(z, bb)]
    return z


def _gdn_prepare(q, k, v, col, row, d):
    g_idx = d * N_HEADS
    b_idx = 8 + d * N_HEADS
    t_idx = 16 + d * N_HEADS
    qs, kbs, ks, rhs_v, rhs_k, gcs, kds, e_tot = [], [], [], [], [], [], [], []
    for h in range(N_HEADS):
        sl = slice(h * D_HEAD, (h + 1) * D_HEAD)
        gc = col[:, g_idx + h:g_idx + h + 1]
        bc = col[:, b_idx + h:b_idx + h + 1]
        gt = col[:, t_idx + h:t_idx + h + 1]
        eg = jnp.exp(gc)
        k_h = k[:, sl]
        kb = k_h * bc
        qs.append(q[:, sl] * eg)
        kbs.append(kb)
        ks.append(k_h)
        rhs_v.append(v[:, sl] * bc)
        rhs_k.append(kb * eg)
        gcs.append(jnp.broadcast_to(gc, (GDN_CHUNK, N_HEADS * GDN_CHUNK)))
        kds.append((k_h * jnp.exp(gt - gc)).astype(BF16))
        e_tot.append(jnp.exp(row[8 + g_idx + h:8 + g_idx + h + 1, :D_HEAD]))
    return dict(
        gram_lhs=jnp.concatenate(kbs + [_stack_heads(q)], axis=0).astype(BF16),
        k_st=jnp.concatenate(ks, axis=0).astype(BF16),
        dlog=jnp.concatenate(gcs, axis=0) - row[d:d + 1, :],
        rhs=jnp.concatenate([jnp.concatenate(rhs_v, axis=0), jnp.concatenate(rhs_k, axis=0)], axis=1),
        qs=jnp.concatenate(qs, axis=0), kds=kds, e_tot=e_tot)


def _gdn_affine_terms(chains, masks):
    n = N_HEADS * GDN_CHUNK
    gram = [lax.dot_general(c["gram_lhs"], c["k_st"], (((1,), (1,)), ((), ())),
                            preferred_element_type=F32) for c in chains]
    decay = [jnp.exp(jnp.where(m[0], c["dlog"], NEG)) for c, m in zip(chains, masks)]
    a_mats = [jnp.where(m[1], g[:n] * dc, 0.0) for g, dc, m in zip(gram, decay, masks)]
    attn_b = _bf([g[n:] * dc for g, dc in zip(gram, decay)])
    xb = _bf(_unit_tri_solve(a_mats, [c["rhs"] for c in chains], [m[2] for m in masks]))
    ax = [_mm(a, x) for a, x in zip(attn_b, xb)]
    out = []
    for c, x, a in zip(chains, xb, ax):
        heads = []
        for h in range(N_HEADS):
            rs = slice(h * GDN_CHUNK, (h + 1) * GDN_CHUNK)
            kx = lax.dot_general(c["kds"][h], x[rs], (((0,), (0,)), ((), ())),
                                 preferred_element_type=F32)
            p = c["qs"][rs] - a[rs, D_HEAD:]
            heads.append(dict(lhs=jnp.concatenate([kx[:, D_HEAD:], p], axis=0).astype(BF16),
                              n=kx[:, :D_HEAD], r=a[rs, :D_HEAD], e=c["e_tot"][h]))
        out.append(heads)
    return out


def _gdn_kernel(qf_ref, kf_ref, vf_ref, colf_ref, rowf_ref,
                qb_ref, kb_ref, vb_ref, colb_ref, rowb_ref,
                of_ref, ob_ref, s_ref):
    nc = qf_ref.shape[0]

    @pl.when(pl.program_id(1) == 0)
    def _():
        s_ref[...] = jnp.zeros(s_ref.shape, F32)

    masks_f = _gdn_masks(False)
    masks_b = _gdn_masks(True)
    chains, masks = [], []
    for c in range(nc):
        cb = nc - 1 - c
        chains.append(_gdn_prepare(qf_ref[c], kf_ref[c], vf_ref[c], colf_ref[c], rowf_ref[c], 0))
        chains.append(_gdn_prepare(qb_ref[cb], kb_ref[cb], vb_ref[cb], colb_ref[cb], rowb_ref[cb], 1))
        masks += [masks_f, masks_b]
    terms = _gdn_affine_terms(chains, masks)

    state = [[s_ref[d, h] for h in range(N_HEADS)] for d in range(2)]
    for c in range(nc):
        ls = [[_mm(terms[2 * c + d][h]["lhs"], state[d][h].astype(BF16)) for h in range(N_HEADS)]
              for d in range(2)]
        for d, o_ref, idx in ((0, of_ref, c), (1, ob_ref, nc - 1 - c)):
            outs = []
            for h in range(N_HEADS):
                t = terms[2 * c + d][h]
                state[d][h] = state[d][h] * t["e"] - ls[d][h][:D_HEAD] + t["n"]
                outs.append(ls[d][h][D_HEAD:] + t["r"])
            o_ref[idx] = jnp.concatenate(outs, axis=1)
    for d in range(2):
        for h in range(N_HEADS):
            s_ref[d, h] = state[d][h]


def _gdn(qkv, col, row):
    B, N, C, _ = qkv.shape
    nc = min(4, N)
    nblk = N // nc
    fwd = lambda j: (lambda b, i: (b, i, 0, j))
    bwd = lambda j: (lambda b, i: (b, nblk - 1 - i, 0, j))
    specs = []
    for mk in (fwd, bwd):
        specs += [pl.BlockSpec((None, nc, C, W_BR), mk(0)),
                  pl.BlockSpec((None, nc, C, W_BR), mk(1)),
                  pl.BlockSpec((None, nc, C, W_BR), mk(2)),
                  pl.BlockSpec((None, nc, C, LANES), mk(0)),
                  pl.BlockSpec((None, nc, 16, 2 * LANES), mk(0))]
    out_sds = jax.ShapeDtypeStruct((B, N, C, W_BR), F32)
    return pl.pallas_call(
        _gdn_kernel,
        out_shape=(out_sds, out_sds),
        grid=(B, nblk),
        in_specs=specs,
        out_specs=(pl.BlockSpec((None, nc, C, W_BR), fwd(0)),
                   pl.BlockSpec((None, nc, C, W_BR), bwd(0))),
        scratch_shapes=[pltpu.VMEM((2, N_HEADS, D_HEAD, D_HEAD), F32)],
        compiler_params=_cparams("parallel", "arbitrary"),
        name="gdn",
    )(qkv, qkv, qkv, col, row, qkv, qkv, qkv, col, row)


def _diff_attn_kernel(q_ref, k_ref, vt_ref, z_ref, lp_ref, nw_ref, y_ref, *, lambda_init):
    tq = q_ref.shape[0]
    sub = min(DIFF_SUB, tq)
    n_keys = k_ref.shape[0]
    ck = min(DIFF_KEYS, n_keys)
    lane = lax.broadcasted_iota(jnp.int32, (sub, D_HEAD), 1)
    lp = lp_ref[...]
    lam = (jnp.exp(jnp.sum(lp[0:1] * lp[1:2], axis=-1, keepdims=True))
           - jnp.exp(jnp.sum(lp[2:3] * lp[3:4], axis=-1, keepdims=True)) + lambda_init)
    streams = [(j, first) for j in range(tq // sub) for first in (True, False)]
    q_masked = []
    for j, first in streams:
        q = q_ref[j * sub:(j + 1) * sub, :]
        q_masked.append(jnp.where((lane < DQK) == first, q, jnp.zeros_like(q)))

    def scores(c, t):
        k_c = k_ref[c * ck:(c + 1) * ck, :]
        return lax.dot_general(k_c, q_masked[t], (((1,), (1,)), ((), ())),
                               preferred_element_type=F32)

    run_max = [None] * len(streams)
    acc = [None] * len(streams)
    pending = [scores(0, t) for t in range(len(streams))]
    for c in range(n_keys // ck):
        vt_c = vt_ref[:, c * ck:(c + 1) * ck]
        for t in range(len(streams)):
            s = pending[t]
            if (c + 1) * ck < n_keys:
                pending[t] = scores(c + 1, t)
            m_c = jnp.max(s, axis=0, keepdims=True)
            m_new = m_c if c == 0 else jnp.maximum(run_max[t], m_c)
            p = jnp.exp2(s - m_new).astype(BF16)
            pv = jnp.dot(vt_c, p, preferred_element_type=F32)
            acc[t] = pv if c == 0 else acc[t] * jnp.exp2(run_max[t] - m_new) + pv
            run_max[t] = m_new

    for j in range(tq // sub):
        halves = []
        for first in (True, False):
            o_aug = acc[streams.index((j, first))]
            inv = 1.0 / o_aug[D_HEAD:D_HEAD + 1, :]
            halves.append(o_aug[:D_HEAD, :] * (inv if first else inv * lam))
        o = (halves[0] - halves[1]).T
        r = lax.rsqrt(jnp.mean(o * o, axis=-1, keepdims=True) + EPS)
        o = o * r * nw_ref[...] * (1.0 - lambda_init)
        rows = slice(j * sub, (j + 1) * sub)
        y_ref[rows, :] = (o * _silu(z_ref[rows, :].astype(F32))).astype(y_ref.dtype)


def _diff_attn(proj, v_t, lam_p, nw, lambda_init):
    B, S, _ = proj.shape
    tq = min(4 * DIFF_SUB, S)
    nh = W_BR // D_HEAD
    return pl.pallas_call(
        functools.partial(_diff_attn_kernel, lambda_init=lambda_init),
        out_shape=jax.ShapeDtypeStruct((B, S, W_BR), BF16),
        grid=(B, N_HEADS, S // tq),
        in_specs=[
            pl.BlockSpec((None, tq, D_HEAD), lambda b, h, i: (b, i, COL_B_Q * nh + h)),
            pl.BlockSpec((None, S, D_HEAD), lambda b, h, i: (b, 0, COL_B_K * nh + h)),
            pl.BlockSpec((None, None, VT_ROWS, S), lambda b, h, i: (b, h, 0, 0)),
            pl.BlockSpec((None, tq, D_HEAD), lambda b, h, i: (b, i, COL_B_Z * nh + h)),
            pl.BlockSpec((4, DQK), lambda b, h, i: (0, 0)),
            pl.BlockSpec((1, D_HEAD), lambda b, h, i: (0, 0)),
        ],
        out_specs=pl.BlockSpec((None, tq, D_HEAD), lambda b, h, i: (b, i, h)),
        compiler_params=_cparams("parallel", "parallel", "parallel"),
        name="diff_attn",
    )(proj, proj, v_t, proj, lam_p, nw)


def _conv_mem_kernel(cb_ref, cc_ref, cx_ref, cz_ref, ccp_ref, cxp_ref, ccn_ref, cxn_ref,
                     mq_ref, mz_ref, kv_ref, cw_ref, y_ref):
    i = pl.program_id(1)
    n_i = pl.num_programs(1)
    T = cb_ref.shape[0]
    prev = jnp.where(i > 0, ccp_ref[HALO - 8:, :].astype(F32) * cxp_ref[HALO - 8:, :].astype(F32), 0.0)
    nxt = jnp.where(i < n_i - 1, ccn_ref[:8, :].astype(F32) * cxn_ref[:8, :].astype(F32), 0.0)
    cur = cc_ref[...].astype(F32) * cx_ref[...].astype(F32)
    xe = jnp.concatenate([prev, cur, nxt], axis=0)
    pad = CONV_W // 2
    acc = xe[8 - pad:8 - pad + T, :] * cw_ref[0:1, :]
    for d in range(1, CONV_W):
        acc = acc + xe[8 - pad + d:8 - pad + d + T, :] * cw_ref[d:d + 1, :]
    y_c = cb_ref[...].astype(F32) * acc * _silu(cz_ref[...].astype(F32))
    y_ref[:, :W_BR] = y_c.astype(y_ref.dtype)

    mz = mz_ref[...].astype(F32)
    for h in range(N_HEADS):
        sl = slice(h * D_HEAD, (h + 1) * D_HEAD)
        s = _dot_nt(mq_ref[:, sl], kv_ref[:, sl]) * (D_HEAD ** -0.5)
        p = jnp.exp(s - jnp.max(s, axis=-1, keepdims=True))
        p = (p * (1.0 / jnp.sum(p, axis=-1, keepdims=True))).astype(BF16)
        o = jnp.dot(p, kv_ref[:, W_BR + h * D_HEAD:W_BR + (h + 1) * D_HEAD],
                    preferred_element_type=F32)
        y_ref[:, W_BR + h * D_HEAD:W_BR + (h + 1) * D_HEAD] = (o * _silu(mz[:, sl])).astype(y_ref.dtype)


def _conv_mem(proj, kv, conv_w):
    B, S, _ = proj.shape
    M = kv.shape[1]
    T = min(1024, S)
    nb = T // HALO
    n_halo = S // HALO
    main = lambda c: pl.BlockSpec((None, T, W_BR), lambda b, i: (b, i, c))
    prev = lambda c: pl.BlockSpec((None, HALO, W_BR), lambda b, i: (b, jnp.maximum(i * nb - 1, 0), c))
    nxt = lambda c: pl.BlockSpec((None, HALO, W_BR),
                                 lambda b, i: (b, jnp.minimum((i + 1) * nb, n_halo - 1), c))
    return pl.pallas_call(
        _conv_mem_kernel,
        out_shape=jax.ShapeDtypeStruct((B, S, 2 * W_BR), BF16),
        grid=(B, S // T),
        in_specs=[main(COL_C_B), main(COL_C_C), main(COL_C_X), main(COL_C_Z),
                  prev(COL_C_C), prev(COL_C_X), nxt(COL_C_C), nxt(COL_C_X),
                  main(COL_M_Q), main(COL_M_Z),
                  pl.BlockSpec((None, M, 2 * W_BR), lambda b, i: (b, 0, 0)),
                  pl.BlockSpec((CONV_W, W_BR), lambda b, i: (0, 0))],
        out_specs=pl.BlockSpec((None, T, 2 * W_BR), lambda b, i: (b, i, 0)),
        compiler_params=_cparams("parallel", "parallel"),
        name="conv_mem",
    )(proj, proj, proj, proj, proj, proj, proj, proj, proj, proj, kv, conv_w)


def _out_proj_kernel(of_ref, ob_ref, az_ref, gnw_ref, yb_ref, ycm_ref, w_ref, x_ref, nw_ref, o_ref):
    y = jnp.dot(jnp.concatenate([yb_ref[...], ycm_ref[...]], axis=1), w_ref[W_BR:, :],
                preferred_element_type=F32)
    o = of_ref[...] + ob_ref[...]
    z = az_ref[...].astype(F32)
    heads = []
    for h in range(N_HEADS):
        sl = slice(h * D_HEAD, (h + 1) * D_HEAD)
        t = o[:, sl]
        r = lax.rsqrt(jnp.mean(t * t, axis=-1, keepdims=True) + EPS)
        heads.append((t * r * gnw_ref[...] * _silu(z[:, sl])).astype(BF16))
    y = y + jnp.dot(jnp.concatenate(heads, axis=1), w_ref[:W_BR, :], preferred_element_type=F32)
    r = lax.rsqrt(jnp.mean(y * y, axis=-1, keepdims=True) + EPS)
    o_ref[...] = x_ref[...] + y * r * nw_ref[...]


def _out_proj(o_f, o_b, proj, gdn_nw, y_b, y_cm, w_out, x, nw):
    B, S, D = x.shape
    T = min(512, S)
    return pl.pallas_call(
        _out_proj_kernel,
        out_shape=jax.ShapeDtypeStruct((B, S, D), F32),
        grid=(B, S // T),
        in_specs=[
            pl.BlockSpec((None, T, W_BR), lambda b, i: (b, i, 0)),
            pl.BlockSpec((None, T, W_BR), lambda b, i: (b, i, 0)),
            pl.BlockSpec((None, T, W_BR), lambda b, i: (b, i, COL_A_Z)),
            pl.BlockSpec((1, D_HEAD), lambda b, i: (0, 0)),
            pl.BlockSpec((None, T, W_BR), lambda b, i: (b, i, 0)),
            pl.BlockSpec((None, T, 2 * W_BR), lambda b, i: (b, i, 0)),
            pl.BlockSpec((4 * W_BR, D), lambda b, i: (0, 0)),
            pl.BlockSpec((None, T, D), lambda b, i: (b, i, 0)),
            pl.BlockSpec((1, D), lambda b, i: (0, 0)),
        ],
        out_specs=pl.BlockSpec((None, T, D), lambda b, i: (b, i, 0)),
        compiler_params=_cparams("parallel", "parallel"),
        name="out_proj",
    )(o_f, o_b, proj, gdn_nw, y_b, y_cm, w_out, x, nw)


def _rope_tables(S):
    half = ROT_DIM // 2
    inv = ROPE_THETA ** (-jnp.arange(0, ROT_DIM, 2, dtype=F32) / ROT_DIM)
    ang = jnp.arange(S, dtype=F32)[:, None] * inv[None, :]
    cos, sin = jnp.cos(ang), jnp.sin(ang)
    ones = jnp.ones((S, DQK - ROT_DIM), F32)
    zeros = jnp.zeros((S, DQK - ROT_DIM), F32)
    z8 = jnp.zeros((S, half), F32)
    c_map = jnp.concatenate([cos, cos, ones], axis=1)
    sa_map = jnp.concatenate([z8, sin, zeros], axis=1)
    sb_map = jnp.concatenate([-sin, z8, zeros], axis=1)
    two = lambda t: jnp.concatenate([t, t], axis=1)
    return two(c_map), two(sa_map), two(sb_map)


def _row_info(col, N):
    B = col.shape[0]
    c4 = col.reshape(B, N, GDN_CHUNK, LANES)
    g_st = jnp.swapaxes(c4[..., 0:8], 2, 3).reshape(B, N, 2, N_HEADS * GDN_CHUNK)
    tot = jnp.broadcast_to(c4[:, :, 0, 16:24][..., None], (B, N, 8, N_HEADS * GDN_CHUNK))
    pad = jnp.zeros((B, N, 6, N_HEADS * GDN_CHUNK), F32)
    return jnp.concatenate([g_st, pad, tot], axis=2)


def _prep_layer_weights(l, norm_pre, norm_post, norm_mem, w_in, gdn_conv, gdn_A_log, gdn_dt_bias,
                        gdn_norm, diff_lambda, diff_norm, conv_w, w_mem_kv, w_out):
    w = w_in[l]
    q_end = 3 * W_BR
    n_db = 2 * N_HEADS
    dec = w[:, q_end:q_end + n_db]
    bet = w[:, q_end + n_db:q_end + 2 * n_db]
    w_main = jnp.concatenate([w[:, :q_end], w[:, q_end + 2 * n_db:]], axis=1).astype(BF16)
    w_gb = jnp.concatenate([dec, bet, dec, jnp.zeros((w.shape[0], LANES - 3 * n_db), F32)],
                           axis=1).astype(BF16)
    lane_pad = lambda t: jnp.concatenate(
        [t.reshape(1, n_db), jnp.zeros((1, n_db), F32), t.reshape(1, n_db),
         jnp.zeros((1, LANES - 3 * n_db), F32)], axis=1)
    return dict(
        norm_pre=norm_pre[l][None, :], norm_post=norm_post[l][None, :], norm_mem=norm_mem[l][None, :],
        w_main=w_main, w_gb=w_gb, alog=lane_pad(gdn_A_log[l]), dtb=lane_pad(gdn_dt_bias[l]),
        gdn_conv=gdn_conv[l], gdn_norm=gdn_norm[l][None, :], diff_lambda=diff_lambda[l],
        diff_norm=diff_norm[l][None, :], conv_w=conv_w[l], w_kv=w_mem_kv[l].astype(BF16),
        w_out=w_out[l].astype(BF16))


def _layer(x, mem, p, lambda_init, tables):
    B, S, _ = x.shape
    N = S // GDN_CHUNK
    proj, gb, v_t = _in_proj(x, p["norm_pre"], p["w_main"], p["w_gb"], p["alog"], p["dtb"], *tables)
    kv = _kv_proj(mem, p["norm_mem"], p["w_kv"])
    qkv, col = _gdn_prep(proj, gb, p["gdn_conv"])
    row = _row_info(col, N)
    o_f, o_b = _gdn(qkv.reshape(B, N, GDN_CHUNK, 3 * W_BR), col.reshape(B, N, GDN_CHUNK, LANES), row)
    y_b = _diff_attn(proj, v_t, p["diff_lambda"], p["diff_norm"], lambda_init)
    y_cm = _conv_mem(proj, kv, p["conv_w"])
    return _out_proj(o_f.reshape(B, S, W_BR), o_b.reshape(B, S, W_BR), proj, p["gdn_norm"],
                     y_b, y_cm, p["w_out"], x, p["norm_post"])


def _trunk(x, mem, layer_params):
    tables = _rope_tables(x.shape[1])
    for l, p in enumerate(layer_params):
        lambda_init = 0.8 - 0.6 * math.exp(-0.3 * l)
        x = _layer(x, mem, p, lambda_init, tables)
    return x


def kernel(x_prompt, x_sample, mem_prompt, mem_sample, norm_pre, norm_post, norm_mem, w_in, gdn_conv,
           gdn_A_log, gdn_dt_bias, gdn_norm, diff_lambda, diff_norm, conv_w, w_mem_kv, w_out):
    depth = w_in.shape[0]
    params = [_prep_layer_weights(l, norm_pre, norm_post, norm_mem, w_in, gdn_conv, gdn_A_log,
                                  gdn_dt_bias, gdn_norm, diff_lambda, diff_norm, conv_w, w_mem_kv, w_out)
              for l in range(depth)]
    return (_trunk(x_prompt, mem_prompt, params), _trunk(x_sample, mem_sample, params))
```

```python
import functools
import math

import jax
import jax.numpy as jnp
from jax import lax
from jax.experimental import pallas as pl
from jax.experimental.pallas import tpu as pltpu

F32 = jnp.float32
BF16 = jnp.bfloat16
EPS = 1e-6

W_BR = 512
N_HEADS = 4
D_HEAD = 128
DQK = 64
ROT_DIM = 16
ROPE_THETA = 500000.0
GDN_CONV = 5
GDN_CHUNK = 64
CONV_W = 3
N_MAIN = 14 * W_BR
LANES = 128
HALO = 16
VT_ROWS = D_HEAD + 16
GB_LANES = 32
CONV_ROWS = 128
DIFF_SUB = 256
DIFF_KEYS = 512
NEG = -1e30
VMEM_LIMIT = 56 * 1024 * 1024

COL_A_Z = 3
COL_B_Q, COL_B_K, COL_B_V, COL_B_Z = 4, 5, 6, 7
COL_C_B, COL_C_C, COL_C_X, COL_C_Z = 8, 9, 10, 11
COL_M_Q, COL_M_Z = 12, 13
IN_TILE_COLS = 2 * W_BR
IN_NORM_ROWS = 256
IN_TILE_QK = COL_B_Q // 2
IN_TILE_VZ = COL_B_V // 2


def _cparams(*sem):
    return pltpu.CompilerParams(dimension_semantics=sem, vmem_limit_bytes=VMEM_LIMIT)


def _sigmoid(x):
    return 1.0 / (1.0 + jnp.exp(-x))


def _silu(x):
    return x * _sigmoid(x)


def _softplus(x):
    return jnp.maximum(x, 0.0) + jnp.log(1.0 + jnp.exp(-jnp.abs(x)))


def _dot(a, b):
    return jnp.dot(a.astype(BF16), b.astype(BF16), preferred_element_type=F32)


def _dot_nt(a, b):
    return lax.dot_general(a.astype(BF16), b.astype(BF16), (((1,), (1,)), ((), ())),
                           preferred_element_type=F32)


def _dot_tn(a, b):
    return lax.dot_general(a.astype(BF16), b.astype(BF16), (((0,), (0,)), ((), ())),
                           preferred_element_type=F32)


def _same_block(r_i, c_i, size):
    shift = size.bit_length() - 1
    return jnp.right_shift(r_i, shift) == jnp.right_shift(c_i, shift)


def _split3(x):
    hi = x.astype(BF16)
    r1 = x - hi.astype(F32)
    mid = r1.astype(BF16)
    lo = (r1 - mid.astype(F32)).astype(BF16)
    return hi, mid, lo


def _in_proj_kernel(x_ref, nw_ref, w_ref, wgb_ref, alog_ref, dtb_ref, c_ref, sa_ref, sb_ref,
                    o_ref, gb_ref, vt_ref, h_ref):
    j = pl.program_id(2)

    @pl.when(j == 0)
    def _():
        n_rows = x_ref.shape[0]
        blk = min(IN_NORM_ROWS, n_rows)
        for r0 in range(0, n_rows, blk):
            rows = slice(r0, r0 + blk)
            x = x_ref[rows, :]
            ms = jnp.mean(x * x, axis=-1, keepdims=True)
            h = (x * lax.rsqrt(ms + EPS) * nw_ref[...]).astype(BF16)
            h_ref[rows, :] = h
            raw = jnp.dot(h, wgb_ref[...], preferred_element_type=F32)
            lane = lax.broadcasted_iota(jnp.int32, raw.shape, 1)
            g = -jnp.exp(alog_ref[...]) * _softplus(raw + dtb_ref[...])
            is_beta = (lane >= 8) & (lane < 16)
            gb_ref[rows, :] = jnp.where(is_beta, _sigmoid(raw), g)
            o_ref[rows, :] = jnp.dot(h, w_ref[...], preferred_element_type=F32).astype(o_ref.dtype)

    def project():
        return jnp.dot(h_ref[...], w_ref[...], preferred_element_type=F32)

    @pl.when(j == IN_TILE_QK)
    def _():
        acc = project()
        c = c_ref[...]
        sa = sa_ref[...]
        sb = sb_ref[...]
        half = ROT_DIM // 2
        q_scale = DQK ** -0.5 * math.log2(math.e)
        for hh in range(2 * N_HEADS):
            sl = slice(hh * D_HEAD, (hh + 1) * D_HEAD)
            t = acc[:, sl]
            y = t * c + pltpu.roll(t, half, 1) * sa + pltpu.roll(t, D_HEAD - half, 1) * sb
            if hh < N_HEADS:
                y = y * q_scale
            o_ref[:, sl] = y.astype(o_ref.dtype)

    @pl.when(j == IN_TILE_VZ)
    def _():
        acc = project()
        o_ref[...] = acc.astype(o_ref.dtype)
        ones = jnp.ones((VT_ROWS - D_HEAD, acc.shape[0]), vt_ref.dtype)
        for hh in range(N_HEADS):
            vt_ref[hh, :D_HEAD, :] = acc[:, hh * D_HEAD:(hh + 1) * D_HEAD].T.astype(vt_ref.dtype)
            vt_ref[hh, D_HEAD:, :] = ones

    @pl.when((j != 0) & (j != IN_TILE_QK) & (j != IN_TILE_VZ))
    def _():
        o_ref[...] = project().astype(o_ref.dtype)


def _in_proj(x, nw, w_main, w_gb, alog, dtb, cos_t, sin_a, sin_b):
    B, S, D = x.shape
    tm = min(1024, S)
    tn = IN_TILE_COLS
    tab = pl.BlockSpec((tm, D_HEAD), lambda b, i, j: (i, 0))
    return pl.pallas_call(
        _in_proj_kernel,
        out_shape=(jax.ShapeDtypeStruct((B, S, N_MAIN), BF16),
                   jax.ShapeDtypeStruct((B, S, LANES), F32),
                   jax.ShapeDtypeStruct((B, N_HEADS, VT_ROWS, S), BF16)),
        grid=(B, S // tm, N_MAIN // tn),
        in_specs=[
            pl.BlockSpec((None, tm, D), lambda b, i, j: (b, i, 0)),
            pl.BlockSpec((1, D), lambda b, i, j: (0, 0)),
            pl.BlockSpec((D, tn), lambda b, i, j: (0, j)),
            pl.BlockSpec((D, LANES), lambda b, i, j: (0, 0)),
            pl.BlockSpec((1, LANES), lambda b, i, j: (0, 0)),
            pl.BlockSpec((1, LANES), lambda b, i, j: (0, 0)),
            tab, tab, tab,
        ],
        out_specs=(pl.BlockSpec((None, tm, tn), lambda b, i, j: (b, i, j)),
                   pl.BlockSpec((None, tm, LANES), lambda b, i, j: (b, i, 0)),
                   pl.BlockSpec((None, N_HEADS, VT_ROWS, tm), lambda b, i, j: (b, 0, 0, i))),
        scratch_shapes=[pltpu.VMEM((tm, D), BF16)],
        compiler_params=_cparams("parallel", "parallel", "arbitrary"),
        name="in_proj",
    )(x, nw, w_main, w_gb, alog, dtb, cos_t, sin_a, sin_b)


def _kv_proj_kernel(x_ref, nw_ref, w_ref, o_ref):
    x = x_ref[...]
    ms = jnp.mean(x * x, axis=-1, keepdims=True)
    h = (x * lax.rsqrt(ms + EPS) * nw_ref[...]).astype(BF16)
    kv = jnp.dot(h, w_ref[...], preferred_element_type=F32)
    n_k = kv.shape[1] // 2
    o_ref[:, :n_k] = (kv[:, :n_k] * (D_HEAD ** -0.5 * math.log2(math.e))).astype(o_ref.dtype)
    o_ref[:, n_k:] = kv[:, n_k:].astype(o_ref.dtype)


def _kv_proj(mem, nw, w_kv):
    B, M, D = mem.shape
    n_out = w_kv.shape[1]
    return pl.pallas_call(
        _kv_proj_kernel,
        out_shape=jax.ShapeDtypeStruct((B, M, n_out), BF16),
        grid=(B,),
        in_specs=[
            pl.BlockSpec((None, M, D), lambda b: (b, 0, 0)),
            pl.BlockSpec((1, D), lambda b: (0, 0)),
            pl.BlockSpec((D, n_out), lambda b: (0, 0)),
        ],
        out_specs=pl.BlockSpec((None, M, n_out), lambda b: (b, 0, 0)),
        compiler_params=_cparams("parallel"),
        name="kv_proj",
    )(mem, nw, w_kv)


def _gdn_prep_kernel(xp_ref, x_ref, xn_ref, gb_ref, cw_ref, cm_ref, sh_ref, qkv_ref, col_ref):
    i = pl.program_id(1)
    n_i = pl.num_programs(1)
    T = x_ref.shape[0]
    pad = GDN_CONV // 2
    prev = jnp.where(i > 0, xp_ref[...], jnp.zeros_like(xp_ref[...]))
    nxt = jnp.where(i < n_i - 1, xn_ref[...], jnp.zeros_like(xn_ref[...]))
    xe = jnp.concatenate([prev, x_ref[...], nxt], axis=0)
    blk = min(CONV_ROWS, T)
    for r0 in range(0, T, blk):
        win = xe[r0:r0 + blk + 2 * HALO, :]
        taps = jnp.dot(sh_ref[...], win, preferred_element_type=F32)
        acc = xe[r0 + HALO:r0 + HALO + blk, :].astype(F32) * cw_ref[pad:pad + 1, :]
        for n, d in enumerate(dd for dd in range(GDN_CONV) if dd != pad):
            acc = acc + taps[n * blk:(n + 1) * blk, :] * cw_ref[d:d + 1, :]
        y = _silu(acc)
        rows = slice(r0, r0 + blk)
        for h in range(2 * N_HEADS):
            sl = slice(h * D_HEAD, (h + 1) * D_HEAD)
            t = y[:, sl]
            r = lax.rsqrt(jnp.sum(t * t, axis=-1, keepdims=True) + EPS)
            if h < N_HEADS:
                r = r * (D_HEAD ** -0.5)
            qkv_ref[rows, sl] = t * r
        qkv_ref[rows, 2 * W_BR:] = y[:, 2 * W_BR:]

        gb = gb_ref[rows, :]
        lane = lax.broadcasted_iota(jnp.int32, gb.shape, 1)
        hi, mid, lo = (p.astype(F32) for p in _split3(jnp.where(lane < GB_LANES, gb, 0.0)))
        packed = hi + pltpu.roll(mid, GB_LANES, 1) + pltpu.roll(lo, 2 * GB_LANES, 1)
        sums = jnp.dot(cm_ref[...], packed.astype(BF16), preferred_element_type=F32)
        sums = (sums + pltpu.roll(sums, LANES - GB_LANES, 1)
                + pltpu.roll(sums, LANES - 2 * GB_LANES, 1))
        col_ref[rows, :] = jnp.where(lane < 4, sums[:blk],
                                     jnp.where(lane < 8, sums[blk:2 * blk],
                                               jnp.where(lane < 16, gb, sums[2 * blk:])))


def _chunk_sum_matrices(T):
    r_i = lax.broadcasted_iota(jnp.int32, (T, T), 0)
    c_i = lax.broadcasted_iota(jnp.int32, (T, T), 1)
    same = _same_block(r_i, c_i, GDN_CHUNK)
    return jnp.concatenate([same & (c_i <= r_i), same & (c_i >= r_i), same], axis=0).astype(BF16)


def _conv_shift_matrix(blk):
    pad = GDN_CONV // 2
    r_i = lax.broadcasted_iota(jnp.int32, (blk, blk + 2 * HALO), 0)
    c_i = lax.broadcasted_iota(jnp.int32, (blk, blk + 2 * HALO), 1)
    return jnp.concatenate([c_i == r_i + HALO + d - pad for d in range(GDN_CONV) if d != pad],
                           axis=0).astype(BF16)


def _gdn_prep(proj, gb, conv_w):
    B, S, _ = proj.shape
    T = min(512, S)
    nb = T // HALO
    n_halo = S // HALO
    wq = 3 * W_BR
    blk = min(CONV_ROWS, T)
    return pl.pallas_call(
        _gdn_prep_kernel,
        out_shape=(jax.ShapeDtypeStruct((B, S, wq), F32),
                   jax.ShapeDtypeStruct((B, S, LANES), F32)),
        grid=(B, S // T),
        in_specs=[
            pl.BlockSpec((None, HALO, wq), lambda b, i: (b, jnp.maximum(i * nb - 1, 0), 0)),
            pl.BlockSpec((None, T, wq), lambda b, i: (b, i, 0)),
            pl.BlockSpec((None, HALO, wq), lambda b, i: (b, jnp.minimum((i + 1) * nb, n_halo - 1), 0)),
            pl.BlockSpec((None, T, LANES), lambda b, i: (b, i, 0)),
            pl.BlockSpec((GDN_CONV, wq), lambda b, i: (0, 0)),
            pl.BlockSpec((3 * blk, blk), lambda b, i: (0, 0)),
            pl.BlockSpec(((GDN_CONV - 1) * blk, blk + 2 * HALO), lambda b, i: (0, 0)),
        ],
        out_specs=(pl.BlockSpec((None, T, wq), lambda b, i: (b, i, 0)),
                   pl.BlockSpec((None, T, LANES), lambda b, i: (b, i, 0))),
        compiler_params=_cparams("parallel", "parallel"),
        name="gdn_prep",
    )(proj, proj, proj, gb, conv_w, _chunk_sum_matrices(blk), _conv_shift_matrix(blk))


def _gdn_masks(reverse):
    n = N_HEADS * GDN_CHUNK
    r_i = lax.broadcasted_iota(jnp.int32, (n, n), 0)
    c_i = lax.broadcasted_iota(jnp.int32, (n, n), 1)
    same_head = _same_block(r_i, c_i, GDN_CHUNK)
    same_blk = _same_block(r_i, c_i, 8)
    if reverse:
        incl = same_head & (c_i >= r_i)
        strict = same_head & (c_i > r_i)
    else:
        incl = same_head & (c_i <= r_i)
        strict = same_head & (c_i < r_i)
    return incl, strict, same_blk


def _stack_heads(x):
    return jnp.concatenate([x[:, h * D_HEAD:(h + 1) * D_HEAD] for h in range(N_HEADS)], axis=0)


def _mm(a, b):
    return jnp.dot(a, b, preferred_element_type=F32)


def _bf(xs):
    return [x.astype(BF16) for x in xs]


def _unit_tri_solve(a_mats, rhss, same_blks):
    n = a_mats[0].shape[0]
    eye = jnp.where(lax.broadcasted_iota(jnp.int32, (n, n), 0)
                    == lax.broadcasted_iota(jnp.int32, (n, n), 1), 1.0, 0.0)
    d1 = [jnp.where(m, a, 0.0) for a, m in zip(a_mats, same_blks)]
    lo_rhs = _bf([jnp.concatenate([a - d, r], axis=1) for a, d, r in zip(a_mats, d1, rhss)])
    d1b = _bf(d1)
    d2 = [_mm(d, d) for d in d1b]
    d2b = _bf(d2)
    d4b = _bf([_mm(d, d) for d in d2b])
    d3 = [_mm(a, b) for a, b in zip(d1b, d2b)]
    p1 = [eye - a + b - c for a, b, c in zip(d1, d2, d3)]
    t8b = _bf([p + _mm(p.astype(BF16), d) for p, d in zip(p1, d4b)])
    by = [_mm(t, w) for t, w in zip(t8b, lo_rhs)]
    b1b = _bf([x[:, :n] for x in by])
    b2b = _bf([_mm(b, b) for b in b1b])
    b4b = _bf([_mm(b, b) for b in b2b])
    z = [x[:, n:] for x in by]
    for bb in (b1b, b2b, b4b):
        sign = -1.0 if bb is b1b else 1.0
        z = [a + sign * _mm(b, a.astype(BF16)) for a, b in zip(z, bb)]
    return z


def _gdn_prepare(q, k, v, col, row, d):
    g_idx = d * N_HEADS
    b_idx = 8 + d * N_HEADS
    t_idx = 16 + d * N_HEADS
    qs, kbs, ks, rhs_v, rhs_k, gcs, kds, e_tot = [], [], [], [], [], [], [], []
    for h in range(N_HEADS):
        sl = slice(h * D_HEAD, (h + 1) * D_HEAD)
        gc = col[:, g_idx + h:g_idx + h + 1]
        bc = col[:, b_idx + h:b_idx + h + 1]
        gt = col[:, t_idx + h:t_idx + h + 1]
        eg = jnp.exp(gc)
        k_h = k[:, sl]
        kb = k_h * bc
        qs.append(q[:, sl] * eg)
        kbs.append(kb)
        ks.append(k_h)
        rhs_v.append(v[:, sl] * bc)
        rhs_k.append(kb * eg)
        gcs.append(jnp.broadcast_to(gc, (GDN_CHUNK, N_HEADS * GDN_CHUNK)))
        kds.append((k_h * jnp.exp(gt - gc)).astype(BF16))
        e_tot.append(jnp.exp(row[8 + g_idx + h:8 + g_idx + h + 1, :D_HEAD]))
    return dict(
        gram_lhs=jnp.concatenate(kbs + [_stack_heads(q)], axis=0).astype(BF16),
        k_st=jnp.concatenate(ks, axis=0).astype(BF16),
        dlog=jnp.concatenate(gcs, axis=0) - row[d:d + 1, :],
        rhs=jnp.concatenate([jnp.concatenate(rhs_v, axis=0), jnp.concatenate(rhs_k, axis=0)], axis=1),
        qs=jnp.concatenate(qs, axis=0), kds=kds, e_tot=e_tot)


def _gdn_affine_terms(chains, masks):
    n = N_HEADS * GDN_CHUNK
    gram = [lax.dot_general(c["gram_lhs"], c["k_st"], (((1,), (1,)), ((), ())),
                            preferred_element_type=F32) for c in chains]
    decay = [jnp.exp(jnp.where(m[0], c["dlog"], NEG)) for c, m in zip(chains, masks)]
    a_mats = [jnp.where(m[1], g[:n] * dc, 0.0) for g, dc, m in zip(gram, decay, masks)]
    attn_b = _bf([g[n:] * dc for g, dc in zip(gram, decay)])
    xb = _bf(_unit_tri_solve(a_mats, [c["rhs"] for c in chains], [m[2] for m in masks]))
    ax = [_mm(a, x) for a, x in zip(attn_b, xb)]
    out = []
    for c, x, a in zip(chains, xb, ax):
        heads = []
        for h in range(N_HEADS):
            rs = slice(h * GDN_CHUNK, (h + 1) * GDN_CHUNK)
            kx = lax.dot_general(c["kds"][h], x[rs], (((0,), (0,)), ((), ())),
                                 preferred_element_type=F32)
            p = c["qs"][rs] - a[rs, D_HEAD:]
            heads.append(dict(lhs=jnp.concatenate([kx[:, D_HEAD:], p], axis=0).astype(BF16),
                              n=kx[:, :D_HEAD], r=a[rs, :D_HEAD], e=c["e_tot"][h]))
        out.append(heads)
    return out


def _gdn_kernel(qf_ref, kf_ref, vf_ref, colf_ref, rowf_ref,
                qb_ref, kb_ref, vb_ref, colb_ref, rowb_ref,
                of_ref, ob_ref, s_ref):
    nc = qf_ref.shape[0]

    @pl.when(pl.program_id(1) == 0)
    def _():
        s_ref[...] = jnp.zeros(s_ref.shape, F32)

    masks_f = _gdn_masks(False)
    masks_b = _gdn_masks(True)
    chains, masks = [], []
    for c in range(nc):
        cb = nc - 1 - c
        chains.append(_gdn_prepare(qf_ref[c], kf_ref[c], vf_ref[c], colf_ref[c], rowf_ref[c], 0))
        chains.append(_gdn_prepare(qb_ref[cb], kb_ref[cb], vb_ref[cb], colb_ref[cb], rowb_ref[cb], 1))
        masks += [masks_f, masks_b]
    terms = _gdn_affine_terms(chains, masks)

    state = [[s_ref[d, h] for h in range(N_HEADS)] for d in range(2)]
    for c in range(nc):
        ls = [[_mm(terms[2 * c + d][h]["lhs"], state[d][h].astype(BF16)) for h in range(N_HEADS)]
              for d in range(2)]
        for d, o_ref, idx in ((0, of_ref, c), (1, ob_ref, nc - 1 - c)):
            outs = []
            for h in range(N_HEADS):
                t = terms[2 * c + d][h]
                state[d][h] = state[d][h] * t["e"] - ls[d][h][:D_HEAD] + t["n"]
                outs.append(ls[d][h][D_HEAD:] + t["r"])
            o_ref[idx] = jnp.concatenate(outs, axis=1)
    for d in range(2):
        for h in range(N_HEADS):
            s_ref[d, h] = state[d][h]


def _gdn(qkv, col, row):
    B, N, C, _ = qkv.shape
    nc = min(4, N)
    nblk = N // nc
    fwd = lambda j: (lambda b, i: (b, i, 0, j))
    bwd = lambda j: (lambda b, i: (b, nblk - 1 - i, 0, j))
    specs = []
    for mk in (fwd, bwd):
        specs += [pl.BlockSpec((None, nc, C, W_BR), mk(0)),
                  pl.BlockSpec((None, nc, C, W_BR), mk(1)),
                  pl.BlockSpec((None, nc, C, W_BR), mk(2)),
                  pl.BlockSpec((None, nc, C, LANES), mk(0)),
                  pl.BlockSpec((None, nc, 16, 2 * LANES), mk(0))]
    out_sds = jax.ShapeDtypeStruct((B, N, C, W_BR), F32)
    return pl.pallas_call(
        _gdn_kernel,
        out_shape=(out_sds, out_sds),
        grid=(B, nblk),
        in_specs=specs,
        out_specs=(pl.BlockSpec((None, nc, C, W_BR), fwd(0)),
                   pl.BlockSpec((None, nc, C, W_BR), bwd(0))),
        scratch_shapes=[pltpu.VMEM((2, N_HEADS, D_HEAD, D_HEAD), F32)],
        compiler_params=_cparams("parallel", "arbitrary"),
        name="gdn",
    )(qkv, qkv, qkv, col, row, qkv, qkv, qkv, col, row)


def _diff_attn_kernel(q_ref, k_ref, vt_ref, z_ref, lp_ref, nw_ref, y_ref, *, lambda_init):
    tq = q_ref.shape[0]
    sub = min(DIFF_SUB, tq)
    n_keys = k_ref.shape[0]
    ck = min(DIFF_KEYS, n_keys)
    lane = lax.broadcasted_iota(jnp.int32, (sub, D_HEAD), 1)
    lp = lp_ref[...]
    lam = (jnp.exp(jnp.sum(lp[0:1] * lp[1:2], axis=-1, keepdims=True))
           - jnp.exp(jnp.sum(lp[2:3] * lp[3:4], axis=-1, keepdims=True)) + lambda_init)
    streams = [(j, first) for j in range(tq // sub) for first in (True, False)]
    q_masked = []
    for j, first in streams:
        q = q_ref[j * sub:(j + 1) * sub, :]
        q_masked.append(jnp.where((lane < DQK) == first, q, jnp.zeros_like(q)))

    def scores(c, t):
        k_c = k_ref[c * ck:(c + 1) * ck, :]
        return lax.dot_general(k_c, q_masked[t], (((1,), (1,)), ((), ())),
                               preferred_element_type=F32)

    run_max = [None] * len(streams)
    acc = [None] * len(streams)
    pending = [scores(0, t) for t in range(len(streams))]
    for c in range(n_keys // ck):
        vt_c = vt_ref[:, c * ck:(c + 1) * ck]
        for t in range(len(streams)):
            s = pending[t]
            if (c + 1) * ck < n_keys:
                pending[t] = scores(c + 1, t)
            m_c = jnp.max(s, axis=0, keepdims=True)
            m_new = m_c if c == 0 else jnp.maximum(run_max[t], m_c)
            p = jnp.exp2(s - m_new).astype(BF16)
            pv = jnp.dot(vt_c, p, preferred_element_type=F32)
            acc[t] = pv if c == 0 else acc[t] * jnp.exp2(run_max[t] - m_new) + pv
            run_max[t] = m_new

    for j in range(tq // sub):
        halves = []
        for first in (True, False):
            o_aug = acc[streams.index((j, first))]
            inv = 1.0 / o_aug[D_HEAD:D_HEAD + 1, :]
            halves.append(o_aug[:D_HEAD, :] * (inv if first else inv * lam))
        o = (halves[0] - halves[1]).T
        r = lax.rsqrt(jnp.mean(o * o, axis=-1, keepdims=True) + EPS)
        o = o * r * nw_ref[...] * (1.0 - lambda_init)
        rows = slice(j * sub, (j + 1) * sub)
        y_ref[rows, :] = (o * _silu(z_ref[rows, :].astype(F32))).astype(y_ref.dtype)


def _diff_attn(proj, v_t, lam_p, nw, lambda_init):
    B, S, _ = proj.shape
    tq = min(4 * DIFF_SUB, S)
    nh = W_BR // D_HEAD
    return pl.pallas_call(
        functools.partial(_diff_attn_kernel, lambda_init=lambda_init),
        out_shape=jax.ShapeDtypeStruct((B, S, W_BR), BF16),
        grid=(B, N_HEADS, S // tq),
        in_specs=[
            pl.BlockSpec((None, tq, D_HEAD), lambda b, h, i: (b, i, COL_B_Q * nh + h)),
            pl.BlockSpec((None, S, D_HEAD), lambda b, h, i: (b, 0, COL_B_K * nh + h)),
            pl.BlockSpec((None, None, VT_ROWS, S), lambda b, h, i: (b, h, 0, 0)),
            pl.BlockSpec((None, tq, D_HEAD), lambda b, h, i: (b, i, COL_B_Z * nh + h)),
            pl.BlockSpec((4, DQK), lambda b, h, i: (0, 0)),
            pl.BlockSpec((1, D_HEAD), lambda b, h, i: (0, 0)),
        ],
        out_specs=pl.BlockSpec((None, tq, D_HEAD), lambda b, h, i: (b, i, h)),
        compiler_params=_cparams("parallel", "parallel", "parallel"),
        name="diff_attn",
    )(proj, proj, v_t, proj, lam_p, nw)


def _conv_mem_kernel(cb_ref, cc_ref, cx_ref, cz_ref, ccp_ref, cxp_ref, ccn_ref, cxn_ref,
                     mq_ref, mz_ref, kv_ref, cw_ref, y_ref):
    i = pl.program_id(1)
    n_i = pl.num_programs(1)
    T = cb_ref.shape[0]
    prev = jnp.where(i > 0, ccp_ref[HALO - 8:, :].astype(F32) * cxp_ref[HALO - 8:, :].astype(F32), 0.0)
    nxt = jnp.where(i < n_i - 1, ccn_ref[:8, :].astype(F32) * cxn_ref[:8, :].astype(F32), 0.0)
    cur = cc_ref[...].astype(F32) * cx_ref[...].astype(F32)
    xe = jnp.concatenate([prev, cur, nxt], axis=0)
    pad = CONV_W // 2
    acc = xe[8 - pad:8 - pad + T, :] * cw_ref[0:1, :]
    for d in range(1, CONV_W):
        acc = acc + xe[8 - pad + d:8 - pad + d + T, :] * cw_ref[d:d + 1, :]
    y_c = cb_ref[...].astype(F32) * acc * _silu(cz_ref[...].astype(F32))
    y_ref[:, :W_BR] = y_c.astype(y_ref.dtype)

    mz = mz_ref[...].astype(F32)
    ones = jnp.ones((kv_ref.shape[0], D_HEAD), kv_ref.dtype)
    for h in range(N_HEADS):
        sl = slice(h * D_HEAD, (h + 1) * D_HEAD)
        s = lax.dot_general(mq_ref[:, sl], kv_ref[:, sl], (((1,), (1,)), ((), ())),
                            preferred_element_type=F32)
        p = jnp.exp2(s - jnp.max(s, axis=-1, keepdims=True)).astype(BF16)
        v_aug = jnp.concatenate([kv_ref[:, W_BR + h * D_HEAD:W_BR + (h + 1) * D_HEAD], ones], axis=1)
        o = jnp.dot(p, v_aug, preferred_element_type=F32)
        o = o[:, :D_HEAD] * (1.0 / o[:, D_HEAD:])
        y_ref[:, W_BR + h * D_HEAD:W_BR + (h + 1) * D_HEAD] = (o * _silu(mz[:, sl])).astype(y_ref.dtype)


def _conv_mem(proj, kv, conv_w):
    B, S, _ = proj.shape
    M = kv.shape[1]
    T = min(1024, S)
    nb = T // HALO
    n_halo = S // HALO
    main = lambda c: pl.BlockSpec((None, T, W_BR), lambda b, i: (b, i, c))
    prev = lambda c: pl.BlockSpec((None, HALO, W_BR), lambda b, i: (b, jnp.maximum(i * nb - 1, 0), c))
    nxt = lambda c: pl.BlockSpec((None, HALO, W_BR),
                                 lambda b, i: (b, jnp.minimum((i + 1) * nb, n_halo - 1), c))
    return pl.pallas_call(
        _conv_mem_kernel,
        out_shape=jax.ShapeDtypeStruct((B, S, 2 * W_BR), BF16),
        grid=(B, S // T),
        in_specs=[main(COL_C_B), main(COL_C_C), main(COL_C_X), main(COL_C_Z),
                  prev(COL_C_C), prev(COL_C_X), nxt(COL_C_C), nxt(COL_C_X),
                  main(COL_M_Q), main(COL_M_Z),
                  pl.BlockSpec((None, M, 2 * W_BR), lambda b, i: (b, 0, 0)),
                  pl.BlockSpec((CONV_W, W_BR), lambda b, i: (0, 0))],
        out_specs=pl.BlockSpec((None, T, 2 * W_BR), lambda b, i: (b, i, 0)),
        compiler_params=_cparams("parallel", "parallel"),
        name="conv_mem",
    )(proj, proj, proj, proj, proj, proj, proj, proj, proj, proj, kv, conv_w)


def _out_proj_kernel(of_ref, ob_ref, az_ref, gnw_ref, yb_ref, ycm_ref, w_ref, x_ref, nw_ref, o_ref):
    y = jnp.dot(jnp.concatenate([yb_ref[...], ycm_ref[...]], axis=1), w_ref[W_BR:, :],
                preferred_element_type=F32)
    o = of_ref[...] + ob_ref[...]
    z = az_ref[...].astype(F32)
    heads = []
    for h in range(N_HEADS):
        sl = slice(h * D_HEAD, (h + 1) * D_HEAD)
        t = o[:, sl]
        r = lax.rsqrt(jnp.mean(t * t, axis=-1, keepdims=True) + EPS)
        heads.append((t * r * gnw_ref[...] * _silu(z[:, sl])).astype(BF16))
    y = y + jnp.dot(jnp.concatenate(heads, axis=1), w_ref[:W_BR, :], preferred_element_type=F32)
    r = lax.rsqrt(jnp.mean(y * y, axis=-1, keepdims=True) + EPS)
    o_ref[...] = x_ref[...] + y * r * nw_ref[...]


def _out_proj(o_f, o_b, proj, gdn_nw, y_b, y_cm, w_out, x, nw):
    B, S, D = x.shape
    T = min(512, S)
    return pl.pallas_call(
        _out_proj_kernel,
        out_shape=jax.ShapeDtypeStruct((B, S, D), F32),
        grid=(B, S // T),
        in_specs=[
            pl.BlockSpec((None, T, W_BR), lambda b, i: (b, i, 0)),
            pl.BlockSpec((None, T, W_BR), lambda b, i: (b, i, 0)),
            pl.BlockSpec((None, T, W_BR), lambda b, i: (b, i, COL_A_Z)),
            pl.BlockSpec((1, D_HEAD), lambda b, i: (0, 0)),
            pl.BlockSpec((None, T, W_BR), lambda b, i: (b, i, 0)),
            pl.BlockSpec((None, T, 2 * W_BR), lambda b, i: (b, i, 0)),
            pl.BlockSpec((4 * W_BR, D), lambda b, i: (0, 0)),
            pl.BlockSpec((None, T, D), lambda b, i: (b, i, 0)),
            pl.BlockSpec((1, D), lambda b, i: (0, 0)),
        ],
        out_specs=pl.BlockSpec((None, T, D), lambda b, i: (b, i, 0)),
        compiler_params=_cparams("parallel", "parallel"),
        name="out_proj",
    )(o_f, o_b, proj, gdn_nw, y_b, y_cm, w_out, x, nw)


def _rope_tables(S):
    half = ROT_DIM // 2
    inv = ROPE_THETA ** (-jnp.arange(0, ROT_DIM, 2, dtype=F32) / ROT_DIM)
    ang = jnp.arange(S, dtype=F32)[:, None] * inv[None, :]
    cos, sin = jnp.cos(ang), jnp.sin(ang)
    ones = jnp.ones((S, DQK - ROT_DIM), F32)
    zeros = jnp.zeros((S, DQK - ROT_DIM), F32)
    z8 = jnp.zeros((S, half), F32)
    c_map = jnp.concatenate([cos, cos, ones], axis=1)
    sa_map = jnp.concatenate([z8, sin, zeros], axis=1)
    sb_map = jnp.concatenate([-sin, z8, zeros], axis=1)
    two = lambda t: jnp.concatenate([t, t], axis=1)
    return two(c_map), two(sa_map), two(sb_map)


def _row_info(col, N):
    B = col.shape[0]
    c4 = col.reshape(B, N, GDN_CHUNK, LANES)
    g_st = jnp.swapaxes(c4[..., 0:8], 2, 3).reshape(B, N, 2, N_HEADS * GDN_CHUNK)
    tot = jnp.broadcast_to(c4[:, :, 0, 16:24][..., None], (B, N, 8, N_HEADS * GDN_CHUNK))
    pad = jnp.zeros((B, N, 6, N_HEADS * GDN_CHUNK), F32)
    return jnp.concatenate([g_st, pad, tot], axis=2)


def _prep_layer_weights(l, norm_pre, norm_post, norm_mem, w_in, gdn_conv, gdn_A_log, gdn_dt_bias,
                        gdn_norm, diff_lambda, diff_norm, conv_w, w_mem_kv, w_out):
    w = w_in[l]
    q_end = 3 * W_BR
    n_db = 2 * N_HEADS
    dec = w[:, q_end:q_end + n_db]
    bet = w[:, q_end + n_db:q_end + 2 * n_db]
    w_main = jnp.concatenate([w[:, :q_end], w[:, q_end + 2 * n_db:]], axis=1).astype(BF16)
    w_gb = jnp.concatenate([dec, bet, dec, jnp.zeros((w.shape[0], LANES - 3 * n_db), F32)],
                           axis=1).astype(BF16)
    lane_pad = lambda t: jnp.concatenate(
        [t.reshape(1, n_db), jnp.zeros((1, n_db), F32), t.reshape(1, n_db),
         jnp.zeros((1, LANES - 3 * n_db), F32)], axis=1)
    return dict(
        norm_pre=norm_pre[l][None, :], norm_post=norm_post[l][None, :], norm_mem=norm_mem[l][None, :],
        w_main=w_main, w_gb=w_gb, alog=lane_pad(gdn_A_log[l]), dtb=lane_pad(gdn_dt_bias[l]),
        gdn_conv=gdn_conv[l], gdn_norm=gdn_norm[l][None, :], diff_lambda=diff_lambda[l],
        diff_norm=diff_norm[l][None, :], conv_w=conv_w[l], w_kv=w_mem_kv[l].astype(BF16),
        w_out=w_out[l].astype(BF16))


def _layer(x, mem, p, lambda_init, tables):
    B, S, _ = x.shape
    N = S // GDN_CHUNK
    proj, gb, v_t = _in_proj(x, p["norm_pre"], p["w_main"], p["w_gb"], p["alog"], p["dtb"], *tables)
    kv = _kv_proj(mem, p["norm_mem"], p["w_kv"])
    qkv, col = _gdn_prep(proj, gb, p["gdn_conv"])
    row = _row_info(col, N)
    o_f, o_b = _gdn(qkv.reshape(B, N, GDN_CHUNK, 3 * W_BR), col.reshape(B, N, GDN_CHUNK, LANES), row)
    y_b = _diff_attn(proj, v_t, p["diff_lambda"], p["diff_norm"], lambda_init)
    y_cm = _conv_mem(proj, kv, p["conv_w"])
    return _out_proj(o_f.reshape(B, S, W_BR), o_b.reshape(B, S, W_BR), proj, p["gdn_norm"],
                     y_b, y_cm, p["w_out"], x, p["norm_post"])


def _trunk(x, mem, layer_params):
    tables = _rope_tables(x.shape[1])
    for l, p in enumerate(layer_params):
        lambda_init = 0.8 - 0.6 * math.exp(-0.3 * l)
        x = _layer(x, mem, p, lambda_init, tables)
    return x


def kernel(x_prompt, x_sample, mem_prompt, mem_sample, norm_pre, norm_post, norm_mem, w_in, gdn_conv,
           gdn_A_log, gdn_dt_bias, gdn_norm, diff_lambda, diff_norm, conv_w, w_mem_kv, w_out):
    depth = w_in.shape[0]
    params = [_prep_layer_weights(l, norm_pre, norm_post, norm_mem, w_in, gdn_conv, gdn_A_log,
                                  gdn_dt_bias, gdn_norm, diff_lambda, diff_norm, conv_w, w_mem_kv, w_out)
              for l in range(depth)]
    return (_trunk(x_prompt, mem_prompt, params), _trunk(x_sample, mem_sample, params))
```

```python
import functools
import math

import jax
import jax.numpy as jnp
from jax import lax
from jax.experimental import pallas as pl
from jax.experimental.pallas import tpu as pltpu

F32 = jnp.float32
BF16 = jnp.bfloat16
EPS = 1e-6

W_BR = 512
N_HEADS = 4
D_HEAD = 128
DQK = 64
ROT_DIM = 16
ROPE_THETA = 500000.0
GDN_CONV = 5
GDN_CHUNK = 64
CONV_W = 3
N_MAIN = 14 * W_BR
LANES = 128
HALO = 16
VT_ROWS = D_HEAD + 16
GB_LANES = 32
CONV_ROWS = 128
DIFF_SUB = 256
DIFF_KEYS = 512
NEG = -1e30
VMEM_LIMIT = 56 * 1024 * 1024

COL_A_Z = 3
COL_B_Q, COL_B_K, COL_B_V, COL_B_Z = 4, 5, 6, 7
COL_C_B, COL_C_C, COL_C_X, COL_C_Z = 8, 9, 10, 11
COL_M_Q, COL_M_Z = 12, 13
IN_TILE_COLS = 2 * W_BR
IN_NORM_ROWS = 256
IN_TILE_QK = COL_B_Q // 2
IN_TILE_VZ = COL_B_V // 2


def _cparams(*sem):
    return pltpu.CompilerParams(dimension_semantics=sem, vmem_limit_bytes=VMEM_LIMIT)


def _sigmoid(x):
    return 1.0 / (1.0 + jnp.exp(-x))


def _silu(x):
    return x * _sigmoid(x)


def _softplus(x):
    return jnp.maximum(x, 0.0) + jnp.log(1.0 + jnp.exp(-jnp.abs(x)))


def _dot(a, b):
    return jnp.dot(a.astype(BF16), b.astype(BF16), preferred_element_type=F32)


def _dot_nt(a, b):
    return lax.dot_general(a.astype(BF16), b.astype(BF16), (((1,), (1,)), ((), ())),
                           preferred_element_type=F32)


def _dot_tn(a, b):
    return lax.dot_general(a.astype(BF16), b.astype(BF16), (((0,), (0,)), ((), ())),
                           preferred_element_type=F32)


def _same_block(r_i, c_i, size):
    shift = size.bit_length() - 1
    return jnp.right_shift(r_i, shift) == jnp.right_shift(c_i, shift)


def _split3(x):
    hi = x.astype(BF16)
    r1 = x - hi.astype(F32)
    mid = r1.astype(BF16)
    lo = (r1 - mid.astype(F32)).astype(BF16)
    return hi, mid, lo


def _in_proj_kernel(x_ref, nw_ref, w_ref, wgb_ref, alog_ref, dtb_ref, c_ref, sa_ref, sb_ref,
                    o_ref, gb_ref, vt_ref, h_ref):
    j = pl.program_id(2)

    @pl.when(j == 0)
    def _():
        n_rows = x_ref.shape[0]
        blk = min(IN_NORM_ROWS, n_rows)
        for r0 in range(0, n_rows, blk):
            rows = slice(r0, r0 + blk)
            x = x_ref[rows, :]
            ms = jnp.mean(x * x, axis=-1, keepdims=True)
            h = (x * lax.rsqrt(ms + EPS) * nw_ref[...]).astype(BF16)
            h_ref[rows, :] = h
            raw = jnp.dot(h, wgb_ref[...], preferred_element_type=F32)
            lane = lax.broadcasted_iota(jnp.int32, raw.shape, 1)
            g = -jnp.exp(alog_ref[...]) * _softplus(raw + dtb_ref[...])
            is_beta = (lane >= 8) & (lane < 16)
            gb_ref[rows, :] = jnp.where(is_beta, _sigmoid(raw), g)
            o_ref[rows, :] = jnp.dot(h, w_ref[...], preferred_element_type=F32).astype(o_ref.dtype)

    def project():
        return jnp.dot(h_ref[...], w_ref[...], preferred_element_type=F32)

    @pl.when(j == IN_TILE_QK)
    def _():
        acc = project()
        c = c_ref[...]
        sa = sa_ref[...]
        sb = sb_ref[...]
        half = ROT_DIM // 2
        q_scale = DQK ** -0.5 * math.log2(math.e)
        for hh in range(2 * N_HEADS):
            sl = slice(hh * D_HEAD, (hh + 1) * D_HEAD)
            t = acc[:, sl]
            y = t * c + pltpu.roll(t, half, 1) * sa + pltpu.roll(t, D_HEAD - half, 1) * sb
            if hh < N_HEADS:
                y = y * q_scale
            o_ref[:, sl] = y.astype(o_ref.dtype)

    @pl.when(j == IN_TILE_VZ)
    def _():
        acc = project()
        o_ref[...] = acc.astype(o_ref.dtype)
        ones = jnp.ones((VT_ROWS - D_HEAD, acc.shape[0]), vt_ref.dtype)
        for hh in range(N_HEADS):
            vt_ref[hh, :D_HEAD, :] = acc[:, hh * D_HEAD:(hh + 1) * D_HEAD].T.astype(vt_ref.dtype)
            vt_ref[hh, D_HEAD:, :] = ones

    @pl.when((j != 0) & (j != IN_TILE_QK) & (j != IN_TILE_VZ))
    def _():
        o_ref[...] = project().astype(o_ref.dtype)


def _in_proj(x, nw, w_main, w_gb, alog, dtb, cos_t, sin_a, sin_b):
    B, S, D = x.shape
    tm = min(1024, S)
    tn = IN_TILE_COLS
    tab = pl.BlockSpec((tm, D_HEAD), lambda b, i, j: (i, 0))
    return pl.pallas_call(
        _in_proj_kernel,
        out_shape=(jax.ShapeDtypeStruct((B, S, N_MAIN), BF16),
                   jax.ShapeDtypeStruct((B, S, LANES), F32),
                   jax.ShapeDtypeStruct((B, N_HEADS, VT_ROWS, S), BF16)),
        grid=(B, S // tm, N_MAIN // tn),
        in_specs=[
            pl.BlockSpec((None, tm, D), lambda b, i, j: (b, i, 0)),
            pl.BlockSpec((1, D), lambda b, i, j: (0, 0)),
            pl.BlockSpec((D, tn), lambda b, i, j: (0, j)),
            pl.BlockSpec((D, LANES), lambda b, i, j: (0, 0)),
            pl.BlockSpec((1, LANES), lambda b, i, j: (0, 0)),
            pl.BlockSpec((1, LANES), lambda b, i, j: (0, 0)),
            tab, tab, tab,
        ],
        out_specs=(pl.BlockSpec((None, tm, tn), lambda b, i, j: (b, i, j)),
                   pl.BlockSpec((None, tm, LANES), lambda b, i, j: (b, i, 0)),
                   pl.BlockSpec((None, N_HEADS, VT_ROWS, tm), lambda b, i, j: (b, 0, 0, i))),
        scratch_shapes=[pltpu.VMEM((tm, D), BF16)],
        compiler_params=_cparams("parallel", "parallel", "arbitrary"),
        name="in_proj",
    )(x, nw, w_main, w_gb, alog, dtb, cos_t, sin_a, sin_b)


def _kv_proj_kernel(x_ref, nw_ref, w_ref, o_ref):
    x = x_ref[...]
    ms = jnp.mean(x * x, axis=-1, keepdims=True)
    h = (x * lax.rsqrt(ms + EPS) * nw_ref[...]).astype(BF16)
    kv = jnp.dot(h, w_ref[...], preferred_element_type=F32)
    n_k = kv.shape[1] // 2
    o_ref[:, :n_k] = (kv[:, :n_k] * (D_HEAD ** -0.5 * math.log2(math.e))).astype(o_ref.dtype)
    o_ref[:, n_k:] = kv[:, n_k:].astype(o_ref.dtype)


def _kv_proj(mem, nw, w_kv):
    B, M, D = mem.shape
    n_out = w_kv.shape[1]
    return pl.pallas_call(
        _kv_proj_kernel,
        out_shape=jax.ShapeDtypeStruct((B, M, n_out), BF16),
        grid=(B,),
        in_specs=[
            pl.BlockSpec((None, M, D), lambda b: (b, 0, 0)),
            pl.BlockSpec((1, D), lambda b: (0, 0)),
            pl.BlockSpec((D, n_out), lambda b: (0, 0)),
        ],
        out_specs=pl.BlockSpec((None, M, n_out), lambda b: (b, 0, 0)),
        compiler_params=_cparams("parallel"),
        name="kv_proj",
    )(mem, nw, w_kv)


def _gdn_prep_kernel(xp_ref, x_ref, xn_ref, gb_ref, cw_ref, cm_ref, sh_ref, qkv_ref, col_ref):
    i = pl.program_id(1)
    n_i = pl.num_programs(1)
    T = x_ref.shape[0]
    pad = GDN_CONV // 2
    prev = jnp.where(i > 0, xp_ref[...], jnp.zeros_like(xp_ref[...]))
    nxt = jnp.where(i < n_i - 1, xn_ref[...], jnp.zeros_like(xn_ref[...]))
    xe = jnp.concatenate([prev, x_ref[...], nxt], axis=0)
    blk = min(CONV_ROWS, T)
    for r0 in range(0, T, blk):
        win = xe[r0:r0 + blk + 2 * HALO, :]
        taps = jnp.dot(sh_ref[...], win, preferred_element_type=F32)
        acc = xe[r0 + HALO:r0 + HALO + blk, :].astype(F32) * cw_ref[pad:pad + 1, :]
        for n, d in enumerate(dd for dd in range(GDN_CONV) if dd != pad):
            acc = acc + taps[n * blk:(n + 1) * blk, :] * cw_ref[d:d + 1, :]
        y = _silu(acc)
        rows = slice(r0, r0 + blk)
        for h in range(2 * N_HEADS):
            sl = slice(h * D_HEAD, (h + 1) * D_HEAD)
            t = y[:, sl]
            r = lax.rsqrt(jnp.sum(t * t, axis=-1, keepdims=True) + EPS)
            if h < N_HEADS:
                r = r * (D_HEAD ** -0.5)
            qkv_ref[rows, sl] = t * r
        qkv_ref[rows, 2 * W_BR:] = y[:, 2 * W_BR:]

        gb = gb_ref[rows, :]
        lane = lax.broadcasted_iota(jnp.int32, gb.shape, 1)
        hi, mid, lo = (p.astype(F32) for p in _split3(jnp.where(lane < GB_LANES, gb, 0.0)))
        packed = hi + pltpu.roll(mid, GB_LANES, 1) + pltpu.roll(lo, 2 * GB_LANES, 1)
        sums = jnp.dot(cm_ref[...], packed.astype(BF16), preferred_element_type=F32)
        sums = (sums + pltpu.roll(sums, LANES - GB_LANES, 1)
                + pltpu.roll(sums, LANES - 2 * GB_LANES, 1))
        col_ref[rows, :] = jnp.where(lane < 4, sums[:blk],
                                     jnp.where(lane < 8, sums[blk:2 * blk],
                                               jnp.where(lane < 16, gb, sums[2 * blk:])))


def _chunk_sum_matrices(T):
    r_i = lax.broadcasted_iota(jnp.int32, (T, T), 0)
    c_i = lax.broadcasted_iota(jnp.int32, (T, T), 1)
    same = _same_block(r_i, c_i, GDN_CHUNK)
    return jnp.concatenate([same & (c_i <= r_i), same & (c_i >= r_i), same], axis=0).astype(BF16)


def _conv_shift_matrix(blk):
    pad = GDN_CONV // 2
    r_i = lax.broadcasted_iota(jnp.int32, (blk, blk + 2 * HALO), 0)
    c_i = lax.broadcasted_iota(jnp.int32, (blk, blk + 2 * HALO), 1)
    return jnp.concatenate([c_i == r_i + HALO + d - pad for d in range(GDN_CONV) if d != pad],
                           axis=0).astype(BF16)


def _gdn_prep(proj, gb, conv_w):
    B, S, _ = proj.shape
    T = min(512, S)
    nb = T // HALO
    n_halo = S // HALO
    wq = 3 * W_BR
    blk = min(CONV_ROWS, T)
    return pl.pallas_call(
        _gdn_prep_kernel,
        out_shape=(jax.ShapeDtypeStruct((B, S, wq), F32),
                   jax.ShapeDtypeStruct((B, S, LANES), F32)),
        grid=(B, S // T),
        in_specs=[
            pl.BlockSpec((None, HALO, wq), lambda b, i: (b, jnp.maximum(i * nb - 1, 0), 0)),
            pl.BlockSpec((None, T, wq), lambda b, i: (b, i, 0)),
            pl.BlockSpec((None, HALO, wq), lambda b, i: (b, jnp.minimum((i + 1) * nb, n_halo - 1), 0)),
            pl.BlockSpec((None, T, LANES), lambda b, i: (b, i, 0)),
            pl.BlockSpec((GDN_CONV, wq), lambda b, i: (0, 0)),
            pl.BlockSpec((3 * blk, blk), lambda b, i: (0, 0)),
            pl.BlockSpec(((GDN_CONV - 1) * blk, blk + 2 * HALO), lambda b, i: (0, 0)),
        ],
        out_specs=(pl.BlockSpec((None, T, wq), lambda b, i: (b, i, 0)),
                   pl.BlockSpec((None, T, LANES), lambda b, i: (b, i, 0))),
        compiler_params=_cparams("parallel", "parallel"),
        name="gdn_prep",
    )(proj, proj, proj, gb, conv_w, _chunk_sum_matrices(blk), _conv_shift_matrix(blk))


def _gdn_masks(reverse):
    n = N_HEADS * GDN_CHUNK
    r_i = lax.broadcasted_iota(jnp.int32, (n, n), 0)
    c_i = lax.broadcasted_iota(jnp.int32, (n, n), 1)
    same_head = _same_block(r_i, c_i, GDN_CHUNK)
    same_blk = _same_block(r_i, c_i, 8)
    if reverse:
        incl = same_head & (c_i >= r_i)
        strict = same_head & (c_i > r_i)
    else:
        incl = same_head & (c_i <= r_i)
        strict = same_head & (c_i < r_i)
    return incl, strict, same_blk


def _stack_heads(x):
    return jnp.concatenate([x[:, h * D_HEAD:(h + 1) * D_HEAD] for h in range(N_HEADS)], axis=0)


def _mm(a, b):
    return jnp.dot(a, b, preferred_element_type=F32)


def _bf(xs):
    return [x.astype(BF16) for x in xs]


def _unit_tri_solve(a_mats, rhss, same_blks):
    n = a_mats[0].shape[0]
    eye = jnp.where(lax.broadcasted_iota(jnp.int32, (n, n), 0)
                    == lax.broadcasted_iota(jnp.int32, (n, n), 1), 1.0, 0.0)
    d1 = [jnp.where(m, a, 0.0) for a, m in zip(a_mats, same_blks)]
    lo_rhs = _bf([jnp.concatenate([a - d, r], axis=1) for a, d, r in zip(a_mats, d1, rhss)])
    d1b = _bf(d1)
    d2 = [_mm(d, d) for d in d1b]
    d2b = _bf(d2)
    d4b = _bf([_mm(d, d) for d in d2b])
    d3 = [_mm(a, b) for a, b in zip(d1b, d2b)]
    p1 = [eye - a + b - c for a, b, c in zip(d1, d2, d3)]
    t8b = _bf([p + _mm(p.astype(BF16), d) for p, d in zip(p1, d4b)])
    by = [_mm(t, w) for t, w in zip(t8b, lo_rhs)]
    b1b = _bf([x[:, :n] for x in by])
    b2b = _bf([_mm(b, b) for b in b1b])
    b4b = _bf([_mm(b, b) for b in b2b])
    z = [x[:, n:] for x in by]
    for bb in (b1b, b2b, b4b):
        sign = -1.0 if bb is b1b else 1.0
        z = [a + sign * _mm(b, a.astype(BF16)) for a, b in zip(z, bb)]
    return z


def _gdn_prepare(q, k, v, col, row, d):
    g_idx = d * N_HEADS
    b_idx = 8 + d * N_HEADS
    t_idx = 16 + d * N_HEADS
    qs, kbs, ks, rhs_v, rhs_k, gcs, kds, e_tot = [], [], [], [], [], [], [], []
    for h in range(N_HEADS):
        sl = slice(h * D_HEAD, (h + 1) * D_HEAD)
        gc = col[:, g_idx + h:g_idx + h + 1]
        bc = col[:, b_idx + h:b_idx + h + 1]
        gt = col[:, t_idx + h:t_idx + h + 1]
        eg = jnp.exp(gc)
        k_h = k[:, sl]
        kb = k_h * bc
        qs.append(q[:, sl] * eg)
        kbs.append(kb)
        ks.append(k_h)
        rhs_v.append(v[:, sl] * bc)
        rhs_k.append(kb * eg)
        gcs.append(jnp.broadcast_to(gc, (GDN_CHUNK, N_HEADS * GDN_CHUNK)))
        kds.append((k_h * jnp.exp(gt - gc)).astype(BF16))
        e_tot.append(jnp.exp(row[8 + g_idx + h:8 + g_idx + h + 1, :D_HEAD]))
    return dict(
        gram_lhs=jnp.concatenate(kbs + [_stack_heads(q)], axis=0).astype(BF16),
        k_st=jnp.concatenate(ks, axis=0).astype(BF16),
        dlog=jnp.concatenate(gcs, axis=0) - row[d:d + 1, :],
        rhs=jnp.concatenate([jnp.concatenate(rhs_v, axis=0), jnp.concatenate(rhs_k, axis=0)], axis=1),
        qs=jnp.concatenate(qs, axis=0), kds=kds, e_tot=e_tot)


def _gdn_affine_terms(chains, masks):
    n = N_HEADS * GDN_CHUNK
    gram = [lax.dot_general(c["gram_lhs"], c["k_st"], (((1,), (1,)), ((), ())),
                            preferred_element_type=F32) for c in chains]
    decay = [jnp.exp(jnp.where(m[0], c["dlog"], NEG)) for c, m in zip(chains, masks)]
    a_mats = [jnp.where(m[1], g[:n] * dc, 0.0) for g, dc, m in zip(gram, decay, masks)]
    attn_b = _bf([g[n:] * dc for g, dc in zip(gram, decay)])
    xb = _bf(_unit_tri_solve(a_mats, [c["rhs"] for c in chains], [m[2] for m in masks]))
    ax = [_mm(a, x) for a, x in zip(attn_b, xb)]
    out = []
    for c, x, a in zip(chains, xb, ax):
        heads = []
        for h in range(N_HEADS):
            rs = slice(h * GDN_CHUNK, (h + 1) * GDN_CHUNK)
            kx = lax.dot_general(c["kds"][h], x[rs], (((0,), (0,)), ((), ())),
                                 preferred_element_type=F32)
            p = c["qs"][rs] - a[rs, D_HEAD:]
            heads.append(dict(lhs=jnp.concatenate([kx[:, D_HEAD:], p], axis=0).astype(BF16),
                              n=kx[:, :D_HEAD], r=a[rs, :D_HEAD], e=c["e_tot"][h]))
        out.append(heads)
    return out


def _gdn_kernel(qf_ref, kf_ref, vf_ref, colf_ref, rowf_ref,
                qb_ref, kb_ref, vb_ref, colb_ref, rowb_ref,
                of_ref, ob_ref, s_ref):
    nc = qf_ref.shape[0]

    @pl.when(pl.program_id(1) == 0)
    def _():
        s_ref[...] = jnp.zeros(s_ref.shape, F32)

    masks_f = _gdn_masks(False)
    masks_b = _gdn_masks(True)
    chains, masks = [], []
    for c in range(nc):
        cb = nc - 1 - c
        chains.append(_gdn_prepare(qf_ref[c], kf_ref[c], vf_ref[c], colf_ref[c], rowf_ref[c], 0))
        chains.append(_gdn_prepare(qb_ref[cb], kb_ref[cb], vb_ref[cb], colb_ref[cb], rowb_ref[cb], 1))
        masks += [masks_f, masks_b]
    terms = _gdn_affine_terms(chains, masks)

    state = [[s_ref[d, h] for h in range(N_HEADS)] for d in range(2)]
    for c in range(nc):
        ls = [[_mm(terms[2 * c + d][h]["lhs"], state[d][h].astype(BF16)) for h in range(N_HEADS)]
              for d in range(2)]
        for d, o_ref, idx in ((0, of_ref, c), (1, ob_ref, nc - 1 - c)):
            outs = []
            for h in range(N_HEADS):
                t = terms[2 * c + d][h]
                state[d][h] = state[d][h] * t["e"] - ls[d][h][:D_HEAD] + t["n"]
                outs.append(ls[d][h][D_HEAD:] + t["r"])
            o_ref[idx] = jnp.concatenate(outs, axis=1)
    for d in range(2):
        for h in range(N_HEADS):
            s_ref[d, h] = state[d][h]


def _gdn(qkv, col, row):
    B, N, C, _ = qkv.shape
    nc = min(8, N)
    nblk = N // nc
    fwd = lambda j: (lambda b, i: (b, i, 0, j))
    bwd = lambda j: (lambda b, i: (b, nblk - 1 - i, 0, j))
    specs = []
    for mk in (fwd, bwd):
        specs += [pl.BlockSpec((None, nc, C, W_BR), mk(0)),
                  pl.BlockSpec((None, nc, C, W_BR), mk(1)),
                  pl.BlockSpec((None, nc, C, W_BR), mk(2)),
                  pl.BlockSpec((None, nc, C, LANES), mk(0)),
                  pl.BlockSpec((None, nc, 16, 2 * LANES), mk(0))]
    out_sds = jax.ShapeDtypeStruct((B, N, C, W_BR), F32)
    return pl.pallas_call(
        _gdn_kernel,
        out_shape=(out_sds, out_sds),
        grid=(B, nblk),
        in_specs=specs,
        out_specs=(pl.BlockSpec((None, nc, C, W_BR), fwd(0)),
                   pl.BlockSpec((None, nc, C, W_BR), bwd(0))),
        scratch_shapes=[pltpu.VMEM((2, N_HEADS, D_HEAD, D_HEAD), F32)],
        compiler_params=_cparams("parallel", "arbitrary"),
        name="gdn",
    )(qkv, qkv, qkv, col, row, qkv, qkv, qkv, col, row)


def _diff_attn_kernel(q_ref, k_ref, vt_ref, z_ref, lp_ref, nw_ref, y_ref, *, lambda_init):
    tq = q_ref.shape[0]
    sub = min(DIFF_SUB, tq)
    n_keys = k_ref.shape[0]
    ck = min(DIFF_KEYS, n_keys)
    lane = lax.broadcasted_iota(jnp.int32, (sub, D_HEAD), 1)
    lp = lp_ref[...]
    lam = (jnp.exp(jnp.sum(lp[0:1] * lp[1:2], axis=-1, keepdims=True))
           - jnp.exp(jnp.sum(lp[2:3] * lp[3:4], axis=-1, keepdims=True)) + lambda_init)
    streams = [(j, first) for j in range(tq // sub) for first in (True, False)]
    q_masked = []
    for j, first in streams:
        q = q_ref[j * sub:(j + 1) * sub, :]
        q_masked.append(jnp.where((lane < DQK) == first, q, jnp.zeros_like(q)))

    def scores(c, t):
        k_c = k_ref[c * ck:(c + 1) * ck, :]
        return lax.dot_general(k_c, q_masked[t], (((1,), (1,)), ((), ())),
                               preferred_element_type=F32)

    run_max = [None] * len(streams)
    acc = [None] * len(streams)
    pending = [scores(0, t) for t in range(len(streams))]
    for c in range(n_keys // ck):
        vt_c = vt_ref[:, c * ck:(c + 1) * ck]
        for t in range(len(streams)):
            s = pending[t]
            if (c + 1) * ck < n_keys:
                pending[t] = scores(c + 1, t)
            m_c = jnp.max(s, axis=0, keepdims=True)
            m_new = m_c if c == 0 else jnp.maximum(run_max[t], m_c)
            p = jnp.exp2(s - m_new).astype(BF16)
            pv = jnp.dot(vt_c, p, preferred_element_type=F32)
            acc[t] = pv if c == 0 else acc[t] * jnp.exp2(run_max[t] - m_new) + pv
            run_max[t] = m_new

    for j in range(tq // sub):
        halves = []
        for first in (True, False):
            o_aug = acc[streams.index((j, first))]
            inv = 1.0 / o_aug[D_HEAD:D_HEAD + 1, :]
            halves.append(o_aug[:D_HEAD, :] * (inv if first else inv * lam))
        o = (halves[0] - halves[1]).T
        r = lax.rsqrt(jnp.mean(o * o, axis=-1, keepdims=True) + EPS)
        o = o * r * nw_ref[...] * (1.0 - lambda_init)
        rows = slice(j * sub, (j + 1) * sub)
        y_ref[rows, :] = (o * _silu(z_ref[rows, :].astype(F32))).astype(y_ref.dtype)


def _diff_attn(proj, v_t, lam_p, nw, lambda_init):
    B, S, _ = proj.shape
    tq = min(4 * DIFF_SUB, S)
    nh = W_BR // D_HEAD
    return pl.pallas_call(
        functools.partial(_diff_attn_kernel, lambda_init=lambda_init),
        out_shape=jax.ShapeDtypeStruct((B, S, W_BR), BF16),
        grid=(B, N_HEADS, S // tq),
        in_specs=[
            pl.BlockSpec((None, tq, D_HEAD), lambda b, h, i: (b, i, COL_B_Q * nh + h)),
            pl.BlockSpec((None, S, D_HEAD), lambda b, h, i: (b, 0, COL_B_K * nh + h)),
            pl.BlockSpec((None, None, VT_ROWS, S), lambda b, h, i: (b, h, 0, 0)),
            pl.BlockSpec((None, tq, D_HEAD), lambda b, h, i: (b, i, COL_B_Z * nh + h)),
            pl.BlockSpec((4, DQK), lambda b, h, i: (0, 0)),
            pl.BlockSpec((1, D_HEAD), lambda b, h, i: (0, 0)),
        ],
        out_specs=pl.BlockSpec((None, tq, D_HEAD), lambda b, h, i: (b, i, h)),
        compiler_params=_cparams("parallel", "parallel", "parallel"),
        name="diff_attn",
    )(proj, proj, v_t, proj, lam_p, nw)


def _conv_mem_kernel(cb_ref, cc_ref, cx_ref, cz_ref, ccp_ref, cxp_ref, ccn_ref, cxn_ref,
                     mq_ref, mz_ref, kv_ref, cw_ref, y_ref):
    i = pl.program_id(1)
    n_i = pl.num_programs(1)
    T = cb_ref.shape[0]
    prev = jnp.where(i > 0, ccp_ref[HALO - 8:, :].astype(F32) * cxp_ref[HALO - 8:, :].astype(F32), 0.0)
    nxt = jnp.where(i < n_i - 1, ccn_ref[:8, :].astype(F32) * cxn_ref[:8, :].astype(F32), 0.0)
    cur = cc_ref[...].astype(F32) * cx_ref[...].astype(F32)
    xe = jnp.concatenate([prev, cur, nxt], axis=0)
    pad = CONV_W // 2
    acc = xe[8 - pad:8 - pad + T, :] * cw_ref[0:1, :]
    for d in range(1, CONV_W):
        acc = acc + xe[8 - pad + d:8 - pad + d + T, :] * cw_ref[d:d + 1, :]
    y_c = cb_ref[...].astype(F32) * acc * _silu(cz_ref[...].astype(F32))
    y_ref[:, :W_BR] = y_c.astype(y_ref.dtype)

    mz = mz_ref[...].astype(F32)
    ones = jnp.ones((kv_ref.shape[0], D_HEAD), kv_ref.dtype)
    for h in range(N_HEADS):
        sl = slice(h * D_HEAD, (h + 1) * D_HEAD)
        s = lax.dot_general(mq_ref[:, sl], kv_ref[:, sl], (((1,), (1,)), ((), ())),
                            preferred_element_type=F32)
        p = jnp.exp2(s - jnp.max(s, axis=-1, keepdims=True)).astype(BF16)
        v_aug = jnp.concatenate([kv_ref[:, W_BR + h * D_HEAD:W_BR + (h + 1) * D_HEAD], ones], axis=1)
        o = jnp.dot(p, v_aug, preferred_element_type=F32)
        o = o[:, :D_HEAD] * (1.0 / o[:, D_HEAD:])
        y_ref[:, W_BR + h * D_HEAD:W_BR + (h + 1) * D_HEAD] = (o * _silu(mz[:, sl])).astype(y_ref.dtype)


def _conv_mem(proj, kv, conv_w):
    B, S, _ = proj.shape
    M = kv.shape[1]
    T = min(1024, S)
    nb = T // HALO
    n_halo = S // HALO
    main = lambda c: pl.BlockSpec((None, T, W_BR), lambda b, i: (b, i, c))
    prev = lambda c: pl.BlockSpec((None, HALO, W_BR), lambda b, i: (b, jnp.maximum(i * nb - 1, 0), c))
    nxt = lambda c: pl.BlockSpec((None, HALO, W_BR),
                                 lambda b, i: (b, jnp.minimum((i + 1) * nb, n_halo - 1), c))
    return pl.pallas_call(
        _conv_mem_kernel,
        out_shape=jax.ShapeDtypeStruct((B, S, 2 * W_BR), BF16),
        grid=(B, S // T),
        in_specs=[main(COL_C_B), main(COL_C_C), main(COL_C_X), main(COL_C_Z),
                  prev(COL_C_C), prev(COL_C_X), nxt(COL_C_C), nxt(COL_C_X),
                  main(COL_M_Q), main(COL_M_Z),
                  pl.BlockSpec((None, M, 2 * W_BR), lambda b, i: (b, 0, 0)),
                  pl.BlockSpec((CONV_W, W_BR), lambda b, i: (0, 0))],
        out_specs=pl.BlockSpec((None, T, 2 * W_BR), lambda b, i: (b, i, 0)),
        compiler_params=_cparams("parallel", "parallel"),
        name="conv_mem",
    )(proj, proj, proj, proj, proj, proj, proj, proj, proj, proj, kv, conv_w)


def _out_proj_kernel(of_ref, ob_ref, az_ref, gnw_ref, yb_ref, ycm_ref, w_ref, x_ref, nw_ref, o_ref):
    o = of_ref[...] + ob_ref[...]
    z = az_ref[...].astype(F32)
    heads = []
    for h in range(N_HEADS):
        sl = slice(h * D_HEAD, (h + 1) * D_HEAD)
        t = o[:, sl]
        r = lax.rsqrt(jnp.mean(t * t, axis=-1, keepdims=True) + EPS)
        heads.append((t * r * gnw_ref[...] * _silu(z[:, sl])).astype(BF16))
    y = jnp.dot(jnp.concatenate([yb_ref[...], ycm_ref[...]] + heads, axis=1), w_ref[...],
                preferred_element_type=F32)
    r = lax.rsqrt(jnp.mean(y * y, axis=-1, keepdims=True) + EPS)
    o_ref[...] = x_ref[...] + y * r * nw_ref[...]


def _out_proj(o_f, o_b, proj, gdn_nw, y_b, y_cm, w_out, x, nw):
    B, S, D = x.shape
    T = min(512, S)
    return pl.pallas_call(
        _out_proj_kernel,
        out_shape=jax.ShapeDtypeStruct((B, S, D), F32),
        grid=(B, S // T),
        in_specs=[
            pl.BlockSpec((None, T, W_BR), lambda b, i: (b, i, 0)),
            pl.BlockSpec((None, T, W_BR), lambda b, i: (b, i, 0)),
            pl.BlockSpec((None, T, W_BR), lambda b, i: (b, i, COL_A_Z)),
            pl.BlockSpec((1, D_HEAD), lambda b, i: (0, 0)),
            pl.BlockSpec((None, T, W_BR), lambda b, i: (b, i, 0)),
            pl.BlockSpec((None, T, 2 * W_BR), lambda b, i: (b, i, 0)),
            pl.BlockSpec((4 * W_BR, D), lambda b, i: (0, 0)),
            pl.BlockSpec((None, T, D), lambda b, i: (b, i, 0)),
            pl.BlockSpec((1, D), lambda b, i: (0, 0)),
        ],
        out_specs=pl.BlockSpec((None, T, D), lambda b, i: (b, i, 0)),
        compiler_params=_cparams("parallel", "parallel"),
        name="out_proj",
    )(o_f, o_b, proj, gdn_nw, y_b, y_cm, w_out, x, nw)


def _rope_tables(S):
    half = ROT_DIM // 2
    inv = ROPE_THETA ** (-jnp.arange(0, ROT_DIM, 2, dtype=F32) / ROT_DIM)
    ang = jnp.arange(S, dtype=F32)[:, None] * inv[None, :]
    cos, sin = jnp.cos(ang), jnp.sin(ang)
    ones = jnp.ones((S, DQK - ROT_DIM), F32)
    zeros = jnp.zeros((S, DQK - ROT_DIM), F32)
    z8 = jnp.zeros((S, half), F32)
    c_map = jnp.concatenate([cos, cos, ones], axis=1)
    sa_map = jnp.concatenate([z8, sin, zeros], axis=1)
    sb_map = jnp.concatenate([-sin, z8, zeros], axis=1)
    two = lambda t: jnp.concatenate([t, t], axis=1)
    return two(c_map), two(sa_map), two(sb_map)


def _row_info(col, N):
    B = col.shape[0]
    c4 = col.reshape(B, N, GDN_CHUNK, LANES)
    g_st = jnp.swapaxes(c4[..., 0:8], 2, 3).reshape(B, N, 2, N_HEADS * GDN_CHUNK)
    tot = jnp.broadcast_to(c4[:, :, 0, 16:24][..., None], (B, N, 8, N_HEADS * GDN_CHUNK))
    pad = jnp.zeros((B, N, 6, N_HEADS * GDN_CHUNK), F32)
    return jnp.concatenate([g_st, pad, tot], axis=2)


def _prep_layer_weights(l, norm_pre, norm_post, norm_mem, w_in, gdn_conv, gdn_A_log, gdn_dt_bias,
                        gdn_norm, diff_lambda, diff_norm, conv_w, w_mem_kv, w_out):
    w = w_in[l]
    q_end = 3 * W_BR
    n_db = 2 * N_HEADS
    dec = w[:, q_end:q_end + n_db]
    bet = w[:, q_end + n_db:q_end + 2 * n_db]
    w_main = jnp.concatenate([w[:, :q_end], w[:, q_end + 2 * n_db:]], axis=1).astype(BF16)
    w_gb = jnp.concatenate([dec, bet, dec, jnp.zeros((w.shape[0], LANES - 3 * n_db), F32)],
                           axis=1).astype(BF16)
    lane_pad = lambda t: jnp.concatenate(
        [t.reshape(1, n_db), jnp.zeros((1, n_db), F32), t.reshape(1, n_db),
         jnp.zeros((1, LANES - 3 * n_db), F32)], axis=1)
    return dict(
        norm_pre=norm_pre[l][None, :], norm_post=norm_post[l][None, :], norm_mem=norm_mem[l][None, :],
        w_main=w_main, w_gb=w_gb, alog=lane_pad(gdn_A_log[l]), dtb=lane_pad(gdn_dt_bias[l]),
        gdn_conv=gdn_conv[l], gdn_norm=gdn_norm[l][None, :], diff_lambda=diff_lambda[l],
        diff_norm=diff_norm[l][None, :], conv_w=conv_w[l], w_kv=w_mem_kv[l].astype(BF16),
        w_out=jnp.concatenate([w_out[l][W_BR:], w_out[l][:W_BR]], axis=0).astype(BF16))


def _layer(x, mem, p, lambda_init, tables):
    B, S, _ = x.shape
    N = S // GDN_CHUNK
    proj, gb, v_t = _in_proj(x, p["norm_pre"], p["w_main"], p["w_gb"], p["alog"], p["dtb"], *tables)
    kv = _kv_proj(mem, p["norm_mem"], p["w_kv"])
    qkv, col = _gdn_prep(proj, gb, p["gdn_conv"])
    row = _row_info(col, N)
    o_f, o_b = _gdn(qkv.reshape(B, N, GDN_CHUNK, 3 * W_BR), col.reshape(B, N, GDN_CHUNK, LANES), row)
    y_b = _diff_attn(proj, v_t, p["diff_lambda"], p["diff_norm"], lambda_init)
    y_cm = _conv_mem(proj, kv, p["conv_w"])
    return _out_proj(o_f.reshape(B, S, W_BR), o_b.reshape(B, S, W_BR), proj, p["gdn_norm"],
                     y_b, y_cm, p["w_out"], x, p["norm_post"])


def _trunk(x, mem, layer_params):
    tables = _rope_tables(x.shape[1])
    for l, p in enumerate(layer_params):
        lambda_init = 0.8 - 0.6 * math.exp(-0.3 * l)
        x = _layer(x, mem, p, lambda_init, tables)
    return x


def kernel(x_prompt, x_sample, mem_prompt, mem_sample, norm_pre, norm_post, norm_mem, w_in, gdn_conv,
           gdn_A_log, gdn_dt_bias, gdn_norm, diff_lambda, diff_norm, conv_w, w_mem_kv, w_out):
    depth = w_in.shape[0]
    params = [_prep_layer_weights(l, norm_pre, norm_post, norm_mem, w_in, gdn_conv, gdn_A_log,
                                  gdn_dt_bias, gdn_norm, diff_lambda, diff_norm, conv_w, w_mem_kv, w_out)
              for l in range(depth)]
    return (_trunk(x_prompt, mem_prompt, params), _trunk(x_sample, mem_sample, params))
```

```python
import functools
import math

import jax
import jax.numpy as jnp
from jax import lax
from jax.experimental import pallas as pl
from jax.experimental.pallas import tpu as pltpu

F32 = jnp.float32
BF16 = jnp.bfloat16
EPS = 1e-6

W_BR = 512
N_HEADS = 4
D_HEAD = 128
DQK = 64
ROT_DIM = 16
ROPE_THETA = 500000.0
GDN_CONV = 5
GDN_CHUNK = 64
CONV_W = 3
N_MAIN = 14 * W_BR
LANES = 128
HALO = 16
VT_ROWS = D_HEAD + 16
GB_LANES = 32
CONV_ROWS = 128
DIFF_SUB = 256
DIFF_KEYS = 512
NEG = -1e30
VMEM_LIMIT = 56 * 1024 * 1024

COL_A_Z = 3
COL_B_Q, COL_B_K, COL_B_V, COL_B_Z = 4, 5, 6, 7
COL_C_B, COL_C_C, COL_C_X, COL_C_Z = 8, 9, 10, 11
COL_M_Q, COL_M_Z = 12, 13
IN_TILE_COLS = 2 * W_BR
IN_NORM_ROWS = 256
IN_TILE_QK = COL_B_Q // 2
IN_TILE_VZ = COL_B_V // 2


def _cparams(*sem):
    return pltpu.CompilerParams(dimension_semantics=sem, vmem_limit_bytes=VMEM_LIMIT)


def _sigmoid(x):
    return 1.0 / (1.0 + jnp.exp2(x * -math.log2(math.e)))


def _silu(x):
    return x * _sigmoid(x)


def _softplus(x):
    return jnp.maximum(x, 0.0) + jnp.log(1.0 + jnp.exp(-jnp.abs(x)))


def _dot(a, b):
    return jnp.dot(a.astype(BF16), b.astype(BF16), preferred_element_type=F32)


def _dot_nt(a, b):
    return lax.dot_general(a.astype(BF16), b.astype(BF16), (((1,), (1,)), ((), ())),
                           preferred_element_type=F32)


def _dot_tn(a, b):
    return lax.dot_general(a.astype(BF16), b.astype(BF16), (((0,), (0,)), ((), ())),
                           preferred_element_type=F32)


def _same_block(r_i, c_i, size):
    shift = size.bit_length() - 1
    return jnp.right_shift(r_i, shift) == jnp.right_shift(c_i, shift)


def _split3(x):
    hi = x.astype(BF16)
    r1 = x - hi.astype(F32)
    mid = r1.astype(BF16)
    lo = (r1 - mid.astype(F32)).astype(BF16)
    return hi, mid, lo


def _in_proj_kernel(x_ref, nw_ref, w_ref, wgb_ref, alog_ref, dtb_ref, c_ref, sa_ref, sb_ref,
                    o_ref, gb_ref, vt_ref, h_ref):
    j = pl.program_id(2)

    @pl.when(j == 0)
    def _():
        n_rows = x_ref.shape[0]
        blk = min(IN_NORM_ROWS, n_rows)
        for r0 in range(0, n_rows, blk):
            rows = slice(r0, r0 + blk)
            x = x_ref[rows, :]
            ms = jnp.mean(x * x, axis=-1, keepdims=True)
            h = (x * lax.rsqrt(ms + EPS) * nw_ref[...]).astype(BF16)
            h_ref[rows, :] = h
            raw = jnp.dot(h, wgb_ref[...], preferred_element_type=F32)
            lane = lax.broadcasted_iota(jnp.int32, raw.shape, 1)
            g = -jnp.exp(alog_ref[...]) * _softplus(raw + dtb_ref[...])
            is_beta = (lane >= 8) & (lane < 16)
            gb_ref[rows, :] = jnp.where(is_beta, _sigmoid(raw), g)
            o_ref[rows, :] = jnp.dot(h, w_ref[...], preferred_element_type=F32).astype(o_ref.dtype)

    def project():
        return jnp.dot(h_ref[...], w_ref[...], preferred_element_type=F32)

    @pl.when(j == IN_TILE_QK)
    def _():
        acc = project()
        c = c_ref[...]
        sa = sa_ref[...]
        sb = sb_ref[...]
        half = ROT_DIM // 2
        q_scale = DQK ** -0.5 * math.log2(math.e)
        for hh in range(2 * N_HEADS):
            sl = slice(hh * D_HEAD, (hh + 1) * D_HEAD)
            t = acc[:, sl]
            y = t * c + pltpu.roll(t, half, 1) * sa + pltpu.roll(t, D_HEAD - half, 1) * sb
            if hh < N_HEADS:
                y = y * q_scale
            o_ref[:, sl] = y.astype(o_ref.dtype)

    @pl.when(j == IN_TILE_VZ)
    def _():
        acc = project()
        o_ref[...] = acc.astype(o_ref.dtype)
        ones = jnp.ones((VT_ROWS - D_HEAD, acc.shape[0]), vt_ref.dtype)
        for hh in range(N_HEADS):
            vt_ref[hh, :D_HEAD, :] = acc[:, hh * D_HEAD:(hh + 1) * D_HEAD].T.astype(vt_ref.dtype)
            vt_ref[hh, D_HEAD:, :] = ones

    @pl.when((j != 0) & (j != IN_TILE_QK) & (j != IN_TILE_VZ))
    def _():
        o_ref[...] = project().astype(o_ref.dtype)


def _in_proj(x, nw, w_main, w_gb, alog, dtb, cos_t, sin_a, sin_b):
    B, S, D = x.shape
    tm = min(1024, S)
    tn = IN_TILE_COLS
    tab = pl.BlockSpec((tm, D_HEAD), lambda b, i, j: (i, 0))
    return pl.pallas_call(
        _in_proj_kernel,
        out_shape=(jax.ShapeDtypeStruct((B, S, N_MAIN), BF16),
                   jax.ShapeDtypeStruct((B, S, LANES), F32),
                   jax.ShapeDtypeStruct((B, N_HEADS, VT_ROWS, S), BF16)),
        grid=(B, S // tm, N_MAIN // tn),
        in_specs=[
            pl.BlockSpec((None, tm, D), lambda b, i, j: (b, i, 0)),
            pl.BlockSpec((1, D), lambda b, i, j: (0, 0)),
            pl.BlockSpec((D, tn), lambda b, i, j: (0, j)),
            pl.BlockSpec((D, LANES), lambda b, i, j: (0, 0)),
            pl.BlockSpec((1, LANES), lambda b, i, j: (0, 0)),
            pl.BlockSpec((1, LANES), lambda b, i, j: (0, 0)),
            tab, tab, tab,
        ],
        out_specs=(pl.BlockSpec((None, tm, tn), lambda b, i, j: (b, i, j)),
                   pl.BlockSpec((None, tm, LANES), lambda b, i, j: (b, i, 0)),
                   pl.BlockSpec((None, N_HEADS, VT_ROWS, tm), lambda b, i, j: (b, 0, 0, i))),
        scratch_shapes=[pltpu.VMEM((tm, D), BF16)],
        compiler_params=_cparams("parallel", "parallel", "arbitrary"),
        name="in_proj",
    )(x, nw, w_main, w_gb, alog, dtb, cos_t, sin_a, sin_b)


def _kv_proj_kernel(x_ref, nw_ref, w_ref, o_ref):
    x = x_ref[...]
    ms = jnp.mean(x * x, axis=-1, keepdims=True)
    h = (x * lax.rsqrt(ms + EPS) * nw_ref[...]).astype(BF16)
    kv = jnp.dot(h, w_ref[...], preferred_element_type=F32)
    n_k = kv.shape[1] // 2
    o_ref[:, :n_k] = (kv[:, :n_k] * (D_HEAD ** -0.5 * math.log2(math.e))).astype(o_ref.dtype)
    o_ref[:, n_k:] = kv[:, n_k:].astype(o_ref.dtype)


def _kv_proj(mem, nw, w_kv):
    B, M, D = mem.shape
    n_out = w_kv.shape[1]
    return pl.pallas_call(
        _kv_proj_kernel,
        out_shape=jax.ShapeDtypeStruct((B, M, n_out), BF16),
        grid=(B,),
        in_specs=[
            pl.BlockSpec((None, M, D), lambda b: (b, 0, 0)),
            pl.BlockSpec((1, D), lambda b: (0, 0)),
            pl.BlockSpec((D, n_out), lambda b: (0, 0)),
        ],
        out_specs=pl.BlockSpec((None, M, n_out), lambda b: (b, 0, 0)),
        compiler_params=_cparams("parallel"),
        name="kv_proj",
    )(mem, nw, w_kv)


def _gdn_prep_kernel(xp_ref, x_ref, xn_ref, gb_ref, cw_ref, cm_ref, sh_ref, qkv_ref, col_ref):
    i = pl.program_id(1)
    n_i = pl.num_programs(1)
    T = x_ref.shape[0]
    pad = GDN_CONV // 2
    prev = jnp.where(i > 0, xp_ref[...], jnp.zeros_like(xp_ref[...]))
    nxt = jnp.where(i < n_i - 1, xn_ref[...], jnp.zeros_like(xn_ref[...]))
    xe = jnp.concatenate([prev, x_ref[...], nxt], axis=0)
    blk = min(CONV_ROWS, T)
    for r0 in range(0, T, blk):
        win = xe[r0:r0 + blk + 2 * HALO, :]
        taps = jnp.dot(sh_ref[...], win, preferred_element_type=F32)
        acc = xe[r0 + HALO:r0 + HALO + blk, :].astype(F32) * cw_ref[pad:pad + 1, :]
        for n, d in enumerate(dd for dd in range(GDN_CONV) if dd != pad):
            acc = acc + taps[n * blk:(n + 1) * blk, :] * cw_ref[d:d + 1, :]
        y = _silu(acc)
        rows = slice(r0, r0 + blk)
        for h in range(2 * N_HEADS):
            sl = slice(h * D_HEAD, (h + 1) * D_HEAD)
            t = y[:, sl]
            r = lax.rsqrt(jnp.sum(t * t, axis=-1, keepdims=True) + EPS)
            if h < N_HEADS:
                r = r * (D_HEAD ** -0.5)
            qkv_ref[rows, sl] = t * r
        qkv_ref[rows, 2 * W_BR:] = y[:, 2 * W_BR:]

        gb = gb_ref[rows, :]
        lane = lax.broadcasted_iota(jnp.int32, gb.shape, 1)
        hi, mid, lo = (p.astype(F32) for p in _split3(jnp.where(lane < GB_LANES, gb, 0.0)))
        packed = hi + pltpu.roll(mid, GB_LANES, 1) + pltpu.roll(lo, 2 * GB_LANES, 1)
        sums = jnp.dot(cm_ref[...], packed.astype(BF16), preferred_element_type=F32)
        sums = (sums + pltpu.roll(sums, LANES - GB_LANES, 1)
                + pltpu.roll(sums, LANES - 2 * GB_LANES, 1))
        col_ref[rows, :] = jnp.where(lane < 4, sums[:blk],
                                     jnp.where(lane < 8, sums[blk:2 * blk],
                                               jnp.where(lane < 16, gb, sums[2 * blk:])))


def _chunk_sum_matrices(T):
    r_i = lax.broadcasted_iota(jnp.int32, (T, T), 0)
    c_i = lax.broadcasted_iota(jnp.int32, (T, T), 1)
    same = _same_block(r_i, c_i, GDN_CHUNK)
    return jnp.concatenate([same & (c_i <= r_i), same & (c_i >= r_i), same], axis=0).astype(BF16)


def _conv_shift_matrix(blk):
    pad = GDN_CONV // 2
    r_i = lax.broadcasted_iota(jnp.int32, (blk, blk + 2 * HALO), 0)
    c_i = lax.broadcasted_iota(jnp.int32, (blk, blk + 2 * HALO), 1)
    return jnp.concatenate([c_i == r_i + HALO + d - pad for d in range(GDN_CONV) if d != pad],
                           axis=0).astype(BF16)


def _gdn_prep(proj, gb, conv_w):
    B, S, _ = proj.shape
    T = min(1024, S)
    nb = T // HALO
    n_halo = S // HALO
    wq = 3 * W_BR
    blk = min(CONV_ROWS, T)
    return pl.pallas_call(
        _gdn_prep_kernel,
        out_shape=(jax.ShapeDtypeStruct((B, S, wq), F32),
                   jax.ShapeDtypeStruct((B, S, LANES), F32)),
        grid=(B, S // T),
        in_specs=[
            pl.BlockSpec((None, HALO, wq), lambda b, i: (b, jnp.maximum(i * nb - 1, 0), 0)),
            pl.BlockSpec((None, T, wq), lambda b, i: (b, i, 0)),
            pl.BlockSpec((None, HALO, wq), lambda b, i: (b, jnp.minimum((i + 1) * nb, n_halo - 1), 0)),
            pl.BlockSpec((None, T, LANES), lambda b, i: (b, i, 0)),
            pl.BlockSpec((GDN_CONV, wq), lambda b, i: (0, 0)),
            pl.BlockSpec((3 * blk, blk), lambda b, i: (0, 0)),
            pl.BlockSpec(((GDN_CONV - 1) * blk, blk + 2 * HALO), lambda b, i: (0, 0)),
        ],
        out_specs=(pl.BlockSpec((None, T, wq), lambda b, i: (b, i, 0)),
                   pl.BlockSpec((None, T, LANES), lambda b, i: (b, i, 0))),
        compiler_params=_cparams("parallel", "parallel"),
        name="gdn_prep",
    )(proj, proj, proj, gb, conv_w, _chunk_sum_matrices(blk), _conv_shift_matrix(blk))


def _gdn_masks(reverse):
    n = N_HEADS * GDN_CHUNK
    r_i = lax.broadcasted_iota(jnp.int32, (n, n), 0)
    c_i = lax.broadcasted_iota(jnp.int32, (n, n), 1)
    same_head = _same_block(r_i, c_i, GDN_CHUNK)
    same_blk = _same_block(r_i, c_i, 8)
    if reverse:
        incl = same_head & (c_i >= r_i)
        strict = same_head & (c_i > r_i)
    else:
        incl = same_head & (c_i <= r_i)
        strict = same_head & (c_i < r_i)
    return incl, strict, same_blk


def _stack_heads(x):
    return jnp.concatenate([x[:, h * D_HEAD:(h + 1) * D_HEAD] for h in range(N_HEADS)], axis=0)


def _mm(a, b):
    return jnp.dot(a, b, preferred_element_type=F32)


def _bf(xs):
    return [x.astype(BF16) for x in xs]


def _unit_tri_solve(a_mats, rhss, same_blks):
    n = a_mats[0].shape[0]
    eye = jnp.where(lax.broadcasted_iota(jnp.int32, (n, n), 0)
                    == lax.broadcasted_iota(jnp.int32, (n, n), 1), 1.0, 0.0)
    d1 = [jnp.where(m, a, 0.0) for a, m in zip(a_mats, same_blks)]
    lo_rhs = _bf([jnp.concatenate([a - d, r], axis=1) for a, d, r in zip(a_mats, d1, rhss)])
    d1b = _bf(d1)
    d2 = [_mm(d, d) for d in d1b]
    d2b = _bf(d2)
    d4b = _bf([_mm(d, d) for d in d2b])
    d3 = [_mm(a, b) for a, b in zip(d1b, d2b)]
    p1 = [eye - a + b - c for a, b, c in zip(d1, d2, d3)]
    t8b = _bf([p + _mm(p.astype(BF16), d) for p, d in zip(p1, d4b)])
    by = [_mm(t, w) for t, w in zip(t8b, lo_rhs)]
    b1b = _bf([x[:, :n] for x in by])
    b2b = _bf([_mm(b, b) for b in b1b])
    b4b = _bf([_mm(b, b) for b in b2b])
    z = [x[:, n:] for x in by]
    for bb in (b1b, b2b, b4b):
        sign = -1.0 if bb is b1b else 1.0
        z = [a + sign * _mm(b, a.astype(BF16)) for a, b in zip(z, bb)]
    return z


def _gdn_prepare(q, k, v, col, row, d):
    g_idx = d * N_HEADS
    b_idx = 8 + d * N_HEADS
    t_idx = 16 + d * N_HEADS
    qs, kbs, ks, rhs_v, rhs_k, gcs, kds, e_tot = [], [], [], [], [], [], [], []
    for h in range(N_HEADS):
        sl = slice(h * D_HEAD, (h + 1) * D_HEAD)
        gc = col[:, g_idx + h:g_idx + h + 1]
        bc = col[:, b_idx + h:b_idx + h + 1]
        gt = col[:, t_idx + h:t_idx + h + 1]
        eg = jnp.exp(gc)
        k_h = k[:, sl]
        kb = k_h * bc
        qs.append(q[:, sl] * eg)
        kbs.append(kb)
        ks.append(k_h)
        rhs_v.append(v[:, sl] * bc)
        rhs_k.append(kb * eg)
        gcs.append(jnp.broadcast_to(gc, (GDN_CHUNK, N_HEADS * GDN_CHUNK)))
        kds.append((k_h * jnp.exp(gt - gc)).astype(BF16))
        e_tot.append(jnp.exp(row[8 + g_idx + h:8 + g_idx + h + 1, :D_HEAD]))
    return dict(
        gram_lhs=jnp.concatenate(kbs + [_stack_heads(q)], axis=0).astype(BF16),
        k_st=jnp.concatenate(ks, axis=0).astype(BF16),
        dlog=jnp.concatenate(gcs, axis=0) - row[d:d + 1, :],
        rhs=jnp.concatenate([jnp.concatenate(rhs_v, axis=0), jnp.concatenate(rhs_k, axis=0)], axis=1),
        qs=jnp.concatenate(qs, axis=0), kds=kds, e_tot=e_tot)


def _gdn_affine_terms(chains, masks):
    n = N_HEADS * GDN_CHUNK
    gram = [lax.dot_general(c["gram_lhs"], c["k_st"], (((1,), (1,)), ((), ())),
                            preferred_element_type=F32) for c in chains]
    decay = [jnp.exp(jnp.where(m[0], c["dlog"], NEG)) for c, m in zip(chains, masks)]
    a_mats = [jnp.where(m[1], g[:n] * dc, 0.0) for g, dc, m in zip(gram, decay, masks)]
    attn_b = _bf([g[n:] * dc for g, dc in zip(gram, decay)])
    xb = _bf(_unit_tri_solve(a_mats, [c["rhs"] for c in chains], [m[2] for m in masks]))
    ax = [_mm(a, x) for a, x in zip(attn_b, xb)]
    out = []
    for c, x, a in zip(chains, xb, ax):
        heads = []
        for h in range(N_HEADS):
            rs = slice(h * GDN_CHUNK, (h + 1) * GDN_CHUNK)
            kx = lax.dot_general(c["kds"][h], x[rs], (((0,), (0,)), ((), ())),
                                 preferred_element_type=F32)
            p = c["qs"][rs] - a[rs, D_HEAD:]
            heads.append(dict(lhs=jnp.concatenate([kx[:, D_HEAD:], p], axis=0).astype(BF16),
                              n=kx[:, :D_HEAD], r=a[rs, :D_HEAD], e=c["e_tot"][h]))
        out.append(heads)
    return out


def _gdn_kernel(qf_ref, kf_ref, vf_ref, colf_ref, rowf_ref,
                qb_ref, kb_ref, vb_ref, colb_ref, rowb_ref,
                of_ref, ob_ref, s_ref):
    nc = qf_ref.shape[0]

    @pl.when(pl.program_id(1) == 0)
    def _():
        s_ref[...] = jnp.zeros(s_ref.shape, F32)

    masks_f = _gdn_masks(False)
    masks_b = _gdn_masks(True)
    chains, masks = [], []
    for c in range(nc):
        cb = nc - 1 - c
        chains.append(_gdn_prepare(qf_ref[c], kf_ref[c], vf_ref[c], colf_ref[c], rowf_ref[c], 0))
        chains.append(_gdn_prepare(qb_ref[cb], kb_ref[cb], vb_ref[cb], colb_ref[cb], rowb_ref[cb], 1))
        masks += [masks_f, masks_b]
    terms = _gdn_affine_terms(chains, masks)

    state = [[s_ref[d, h] for h in range(N_HEADS)] for d in range(2)]
    for c in range(nc):
        ls = [[_mm(terms[2 * c + d][h]["lhs"], state[d][h].astype(BF16)) for h in range(N_HEADS)]
              for d in range(2)]
        for d, o_ref, idx in ((0, of_ref, c), (1, ob_ref, nc - 1 - c)):
            outs = []
            for h in range(N_HEADS):
                t = terms[2 * c + d][h]
                state[d][h] = state[d][h] * t["e"] - ls[d][h][:D_HEAD] + t["n"]
                outs.append(ls[d][h][D_HEAD:] + t["r"])
            o_ref[idx] = jnp.concatenate(outs, axis=1)
    for d in range(2):
        for h in range(N_HEADS):
            s_ref[d, h] = state[d][h]


def _gdn(qkv, col, row):
    B, N, C, _ = qkv.shape
    nc = min(8, N)
    nblk = N // nc
    fwd = lambda j: (lambda b, i: (b, i, 0, j))
    bwd = lambda j: (lambda b, i: (b, nblk - 1 - i, 0, j))
    specs = []
    for mk in (fwd, bwd):
        specs += [pl.BlockSpec((None, nc, C, W_BR), mk(0)),
                  pl.BlockSpec((None, nc, C, W_BR), mk(1)),
                  pl.BlockSpec((None, nc, C, W_BR), mk(2)),
                  pl.BlockSpec((None, nc, C, LANES), mk(0)),
                  pl.BlockSpec((None, nc, 16, 2 * LANES), mk(0))]
    out_sds = jax.ShapeDtypeStruct((B, N, C, W_BR), F32)
    return pl.pallas_call(
        _gdn_kernel,
        out_shape=(out_sds, out_sds),
        grid=(B, nblk),
        in_specs=specs,
        out_specs=(pl.BlockSpec((None, nc, C, W_BR), fwd(0)),
                   pl.BlockSpec((None, nc, C, W_BR), bwd(0))),
        scratch_shapes=[pltpu.VMEM((2, N_HEADS, D_HEAD, D_HEAD), F32)],
        compiler_params=_cparams("parallel", "arbitrary"),
        name="gdn",
    )(qkv, qkv, qkv, col, row, qkv, qkv, qkv, col, row)


def _diff_attn_kernel(q_ref, k_ref, vt_ref, z_ref, lp_ref, nw_ref, y_ref, *, lambda_init):
    tq = q_ref.shape[0]
    sub = min(DIFF_SUB, tq)
    n_keys = k_ref.shape[0]
    ck = min(DIFF_KEYS, n_keys)
    lane = lax.broadcasted_iota(jnp.int32, (sub, D_HEAD), 1)
    lp = lp_ref[...]
    lam = (jnp.exp(jnp.sum(lp[0:1] * lp[1:2], axis=-1, keepdims=True))
           - jnp.exp(jnp.sum(lp[2:3] * lp[3:4], axis=-1, keepdims=True)) + lambda_init)
    streams = [(j, first) for j in range(tq // sub) for first in (True, False)]
    q_masked = []
    for j, first in streams:
        q = q_ref[j * sub:(j + 1) * sub, :]
        q_masked.append(jnp.where((lane < DQK) == first, q, jnp.zeros_like(q)))

    def scores(c, t):
        k_c = k_ref[c * ck:(c + 1) * ck, :]
        return lax.dot_general(k_c, q_masked[t], (((1,), (1,)), ((), ())),
                               preferred_element_type=F32)

    run_max = [None] * len(streams)
    acc = [None] * len(streams)
    pending = [scores(0, t) for t in range(len(streams))]
    for c in range(n_keys // ck):
        vt_c = vt_ref[:, c * ck:(c + 1) * ck]
        for t in range(len(streams)):
            s = pending[t]
            if (c + 1) * ck < n_keys:
                pending[t] = scores(c + 1, t)
            m_c = jnp.max(s, axis=0, keepdims=True)
            m_new = m_c if c == 0 else jnp.maximum(run_max[t], m_c)
            p = jnp.exp2(s - m_new).astype(BF16)
            pv = jnp.dot(vt_c, p, preferred_element_type=F32)
            acc[t] = pv if c == 0 else acc[t] * jnp.exp2(run_max[t] - m_new) + pv
            run_max[t] = m_new

    for j in range(tq // sub):
        halves = []
        for first in (True, False):
            o_aug = acc[streams.index((j, first))]
            inv = 1.0 / o_aug[D_HEAD:D_HEAD + 1, :]
            halves.append(o_aug[:D_HEAD, :] * (inv if first else inv * lam))
        o = (halves[0] - halves[1]).T
        r = lax.rsqrt(jnp.mean(o * o, axis=-1, keepdims=True) + EPS)
        o = o * r * nw_ref[...] * (1.0 - lambda_init)
        rows = slice(j * sub, (j + 1) * sub)
        y_ref[rows, :] = (o * _silu(z_ref[rows, :].astype(F32))).astype(y_ref.dtype)


def _diff_attn(proj, v_t, lam_p, nw, lambda_init):
    B, S, _ = proj.shape
    tq = min(8 * DIFF_SUB, S)
    nh = W_BR // D_HEAD
    return pl.pallas_call(
        functools.partial(_diff_attn_kernel, lambda_init=lambda_init),
        out_shape=jax.ShapeDtypeStruct((B, S, W_BR), BF16),
        grid=(B, N_HEADS, S // tq),
        in_specs=[
            pl.BlockSpec((None, tq, D_HEAD), lambda b, h, i: (b, i, COL_B_Q * nh + h)),
            pl.BlockSpec((None, S, D_HEAD), lambda b, h, i: (b, 0, COL_B_K * nh + h)),
            pl.BlockSpec((None, None, VT_ROWS, S), lambda b, h, i: (b, h, 0, 0)),
            pl.BlockSpec((None, tq, D_HEAD), lambda b, h, i: (b, i, COL_B_Z * nh + h)),
            pl.BlockSpec((4, DQK), lambda b, h, i: (0, 0)),
            pl.BlockSpec((1, D_HEAD), lambda b, h, i: (0, 0)),
        ],
        out_specs=pl.BlockSpec((None, tq, D_HEAD), lambda b, h, i: (b, i, h)),
        compiler_params=_cparams("parallel", "parallel", "parallel"),
        name="diff_attn",
    )(proj, proj, v_t, proj, lam_p, nw)


def _conv_mem_kernel(cb_ref, cc_ref, cx_ref, cz_ref, ccp_ref, cxp_ref, ccn_ref, cxn_ref,
                     mq_ref, mz_ref, kv_ref, cw_ref, y_ref):
    i = pl.program_id(1)
    n_i = pl.num_programs(1)
    T = cb_ref.shape[0]
    prev = jnp.where(i > 0, ccp_ref[HALO - 8:, :].astype(F32) * cxp_ref[HALO - 8:, :].astype(F32), 0.0)
    nxt = jnp.where(i < n_i - 1, ccn_ref[:8, :].astype(F32) * cxn_ref[:8, :].astype(F32), 0.0)
    cur = cc_ref[...].astype(F32) * cx_ref[...].astype(F32)
    xe = jnp.concatenate([prev, cur, nxt], axis=0)
    pad = CONV_W // 2
    acc = xe[8 - pad:8 - pad + T, :] * cw_ref[0:1, :]
    for d in range(1, CONV_W):
        acc = acc + xe[8 - pad + d:8 - pad + d + T, :] * cw_ref[d:d + 1, :]
    y_c = cb_ref[...].astype(F32) * acc * _silu(cz_ref[...].astype(F32))
    y_ref[:, :W_BR] = y_c.astype(y_ref.dtype)

    mz = mz_ref[...].astype(F32)
    ones = jnp.ones((kv_ref.shape[0], D_HEAD), kv_ref.dtype)
    for h in range(N_HEADS):
        sl = slice(h * D_HEAD, (h + 1) * D_HEAD)
        s = lax.dot_general(mq_ref[:, sl], kv_ref[:, sl], (((1,), (1,)), ((), ())),
                            preferred_element_type=F32)
        p = jnp.exp2(s - jnp.max(s, axis=-1, keepdims=True)).astype(BF16)
        v_aug = jnp.concatenate([kv_ref[:, W_BR + h * D_HEAD:W_BR + (h + 1) * D_HEAD], ones], axis=1)
        o = jnp.dot(p, v_aug, preferred_element_type=F32)
        o = o[:, :D_HEAD] * (1.0 / o[:, D_HEAD:])
        y_ref[:, W_BR + h * D_HEAD:W_BR + (h + 1) * D_HEAD] = (o * _silu(mz[:, sl])).astype(y_ref.dtype)


def _conv_mem(proj, kv, conv_w):
    B, S, _ = proj.shape
    M = kv.shape[1]
    T = min(1024, S)
    nb = T // HALO
    n_halo = S // HALO
    main = lambda c: pl.BlockSpec((None, T, W_BR), lambda b, i: (b, i, c))
    prev = lambda c: pl.BlockSpec((None, HALO, W_BR), lambda b, i: (b, jnp.maximum(i * nb - 1, 0), c))
    nxt = lambda c: pl.BlockSpec((None, HALO, W_BR),
                                 lambda b, i: (b, jnp.minimum((i + 1) * nb, n_halo - 1), c))
    return pl.pallas_call(
        _conv_mem_kernel,
        out_shape=jax.ShapeDtypeStruct((B, S, 2 * W_BR), BF16),
        grid=(B, S // T),
        in_specs=[main(COL_C_B), main(COL_C_C), main(COL_C_X), main(COL_C_Z),
                  prev(COL_C_C), prev(COL_C_X), nxt(COL_C_C), nxt(COL_C_X),
                  main(COL_M_Q), main(COL_M_Z),
                  pl.BlockSpec((None, M, 2 * W_BR), lambda b, i: (b, 0, 0)),
                  pl.BlockSpec((CONV_W, W_BR), lambda b, i: (0, 0))],
        out_specs=pl.BlockSpec((None, T, 2 * W_BR), lambda b, i: (b, i, 0)),
        compiler_params=_cparams("parallel", "parallel"),
        name="conv_mem",
    )(proj, proj, proj, proj, proj, proj, proj, proj, proj, proj, kv, conv_w)


def _out_proj_kernel(of_ref, ob_ref, az_ref, gnw_ref, yb_ref, ycm_ref, w_ref, x_ref, nw_ref, o_ref):
    o = of_ref[...] + ob_ref[...]
    z = az_ref[...].astype(F32)
    heads = []
    for h in range(N_HEADS):
        sl = slice(h * D_HEAD, (h + 1) * D_HEAD)
        t = o[:, sl]
        r = lax.rsqrt(jnp.mean(t * t, axis=-1, keepdims=True) + EPS)
        heads.append((t * r * gnw_ref[...] * _silu(z[:, sl])).astype(BF16))
    y = jnp.dot(jnp.concatenate([yb_ref[...], ycm_ref[...]] + heads, axis=1), w_ref[...],
                preferred_element_type=F32)
    r = lax.rsqrt(jnp.mean(y * y, axis=-1, keepdims=True) + EPS)
    o_ref[...] = x_ref[...] + y * r * nw_ref[...]


def _out_proj(o_f, o_b, proj, gdn_nw, y_b, y_cm, w_out, x, nw):
    B, S, D = x.shape
    T = min(512, S)
    return pl.pallas_call(
        _out_proj_kernel,
        out_shape=jax.ShapeDtypeStruct((B, S, D), F32),
        grid=(B, S // T),
        in_specs=[
            pl.BlockSpec((None, T, W_BR), lambda b, i: (b, i, 0)),
            pl.BlockSpec((None, T, W_BR), lambda b, i: (b, i, 0)),
            pl.BlockSpec((None, T, W_BR), lambda b, i: (b, i, COL_A_Z)),
            pl.BlockSpec((1, D_HEAD), lambda b, i: (0, 0)),
            pl.BlockSpec((None, T, W_BR), lambda b, i: (b, i, 0)),
            pl.BlockSpec((None, T, 2 * W_BR), lambda b, i: (b, i, 0)),
            pl.BlockSpec((4 * W_BR, D), lambda b, i: (0, 0)),
            pl.BlockSpec((None, T, D), lambda b, i: (b, i, 0)),
            pl.BlockSpec((1, D), lambda b, i: (0, 0)),
        ],
        out_specs=pl.BlockSpec((None, T, D), lambda b, i: (b, i, 0)),
        compiler_params=_cparams("parallel", "parallel"),
        name="out_proj",
    )(o_f, o_b, proj, gdn_nw, y_b, y_cm, w_out, x, nw)


def _rope_tables(S):
    half = ROT_DIM // 2
    inv = ROPE_THETA ** (-jnp.arange(0, ROT_DIM, 2, dtype=F32) / ROT_DIM)
    ang = jnp.arange(S, dtype=F32)[:, None] * inv[None, :]
    cos, sin = jnp.cos(ang), jnp.sin(ang)
    ones = jnp.ones((S, DQK - ROT_DIM), F32)
    zeros = jnp.zeros((S, DQK - ROT_DIM), F32)
    z8 = jnp.zeros((S, half), F32)
    c_map = jnp.concatenate([cos, cos, ones], axis=1)
    sa_map = jnp.concatenate([z8, sin, zeros], axis=1)
    sb_map = jnp.concatenate([-sin, z8, zeros], axis=1)
    two = lambda t: jnp.concatenate([t, t], axis=1)
    return two(c_map), two(sa_map), two(sb_map)


def _row_info(col, N):
    B = col.shape[0]
    c4 = col.reshape(B, N, GDN_CHUNK, LANES)
    g_st = jnp.swapaxes(c4[..., 0:8], 2, 3).reshape(B, N, 2, N_HEADS * GDN_CHUNK)
    tot = jnp.broadcast_to(c4[:, :, 0, 16:24][..., None], (B, N, 8, N_HEADS * GDN_CHUNK))
    pad = jnp.zeros((B, N, 6, N_HEADS * GDN_CHUNK), F32)
    return jnp.concatenate([g_st, pad, tot], axis=2)


def _prep_layer_weights(l, norm_pre, norm_post, norm_mem, w_in, gdn_conv, gdn_A_log, gdn_dt_bias,
                        gdn_norm, diff_lambda, diff_norm, conv_w, w_mem_kv, w_out):
    w = w_in[l]
    q_end = 3 * W_BR
    n_db = 2 * N_HEADS
    dec = w[:, q_end:q_end + n_db]
    bet = w[:, q_end + n_db:q_end + 2 * n_db]
    w_main = jnp.concatenate([w[:, :q_end], w[:, q_end + 2 * n_db:]], axis=1).astype(BF16)
    w_gb = jnp.concatenate([dec, bet, dec, jnp.zeros((w.shape[0], LANES - 3 * n_db), F32)],
                           axis=1).astype(BF16)
    lane_pad = lambda t: jnp.concatenate(
        [t.reshape(1, n_db), jnp.zeros((1, n_db), F32), t.reshape(1, n_db),
         jnp.zeros((1, LANES - 3 * n_db), F32)], axis=1)
    return dict(
        norm_pre=norm_pre[l][None, :], norm_post=norm_post[l][None, :], norm_mem=norm_mem[l][None, :],
        w_main=w_main, w_gb=w_gb, alog=lane_pad(gdn_A_log[l]), dtb=lane_pad(gdn_dt_bias[l]),
        gdn_conv=gdn_conv[l], gdn_norm=gdn_norm[l][None, :], diff_lambda=diff_lambda[l],
        diff_norm=diff_norm[l][None, :], conv_w=conv_w[l], w_kv=w_mem_kv[l].astype(BF16),
        w_out=jnp.concatenate([w_out[l][W_BR:], w_out[l][:W_BR]], axis=0).astype(BF16))


def _layer(x, mem, p, lambda_init, tables):
    B, S, _ = x.shape
    N = S // GDN_CHUNK
    proj, gb, v_t = _in_proj(x, p["norm_pre"], p["w_main"], p["w_gb"], p["alog"], p["dtb"], *tables)
    kv = _kv_proj(mem, p["norm_mem"], p["w_kv"])
    qkv, col = _gdn_prep(proj, gb, p["gdn_conv"])
    row = _row_info(col, N)
    o_f, o_b = _gdn(qkv.reshape(B, N, GDN_CHUNK, 3 * W_BR), col.reshape(B, N, GDN_CHUNK, LANES), row)
    y_b = _diff_attn(proj, v_t, p["diff_lambda"], p["diff_norm"], lambda_init)
    y_cm = _conv_mem(proj, kv, p["conv_w"])
    return _out_proj(o_f.reshape(B, S, W_BR), o_b.reshape(B, S, W_BR), proj, p["gdn_norm"],
                     y_b, y_cm, p["w_out"], x, p["norm_post"])


def _trunk(x, mem, layer_params):
    tables = _rope_tables(x.shape[1])
    for l, p in enumerate(layer_params):
        lambda_init = 0.8 - 0.6 * math.exp(-0.3 * l)
        x = _layer(x, mem, p, lambda_init, tables)
    return x


def kernel(x_prompt, x_sample, mem_prompt, mem_sample, norm_pre, norm_post, norm_mem, w_in, gdn_conv,
           gdn_A_log, gdn_dt_bias, gdn_norm, diff_lambda, diff_norm, conv_w, w_mem_kv, w_out):
    depth = w_in.shape[0]
    params = [_prep_layer_weights(l, norm_pre, norm_post, norm_mem, w_in, gdn_conv, gdn_A_log,
                                  gdn_dt_bias, gdn_norm, diff_lambda, diff_norm, conv_w, w_mem_kv, w_out)
              for l in range(depth)]
    return (_trunk(x_prompt, mem_prompt, params), _trunk(x_sample, mem_sample, params))
```

```python
import functools
import math

import jax
import jax.numpy as jnp
import numpy as np
from jax import lax
from jax.experimental import pallas as pl
from jax.experimental.pallas import tpu as pltpu

F32 = jnp.float32
BF16 = jnp.bfloat16
EPS = 1e-6

W_BR = 512
N_HEADS = 4
D_HEAD = 128
DQK = 64
ROT_DIM = 16
ROPE_THETA = 500000.0
GDN_CONV = 5
GDN_CHUNK = 64
CONV_W = 3
N_MAIN = 14 * W_BR
LANES = 128
HALO = 16
VT_ROWS = D_HEAD + 16
GB_LANES = 32
CONV_ROWS = 128
DIFF_SUB = 256
DIFF_KEYS = 512
NEG = -1e30
VMEM_LIMIT = 56 * 1024 * 1024

COL_A_Z = 3
COL_B_Q, COL_B_K, COL_B_V, COL_B_Z = 4, 5, 6, 7
COL_C_B, COL_C_C, COL_C_X, COL_C_Z = 8, 9, 10, 11
COL_M_Q, COL_M_Z = 12, 13
IN_TILE_COLS = 2 * W_BR
IN_NORM_ROWS = 256
IN_TILE_QK = COL_B_Q // 2
IN_TILE_VZ = COL_B_V // 2


def _cparams(*sem):
    return pltpu.CompilerParams(dimension_semantics=sem, vmem_limit_bytes=VMEM_LIMIT)


def _sigmoid(x):
    return 1.0 / (1.0 + jnp.exp2(x * -math.log2(math.e)))


def _silu(x):
    return x * _sigmoid(x)


def _softplus(x):
    return jnp.maximum(x, 0.0) + jnp.log(1.0 + jnp.exp(-jnp.abs(x)))


def _dot(a, b):
    return jnp.dot(a.astype(BF16), b.astype(BF16), preferred_element_type=F32)


def _dot_nt(a, b):
    return lax.dot_general(a.astype(BF16), b.astype(BF16), (((1,), (1,)), ((), ())),
                           preferred_element_type=F32)


def _dot_tn(a, b):
    return lax.dot_general(a.astype(BF16), b.astype(BF16), (((0,), (0,)), ((), ())),
                           preferred_element_type=F32)


def _same_block(r_i, c_i, size):
    shift = size.bit_length() - 1
    return jnp.right_shift(r_i, shift) == jnp.right_shift(c_i, shift)


def _split3(x):
    hi = x.astype(BF16)
    r1 = x - hi.astype(F32)
    mid = r1.astype(BF16)
    lo = (r1 - mid.astype(F32)).astype(BF16)
    return hi, mid, lo


def _in_proj_kernel(x_ref, nw_ref, w_ref, wgb_ref, alog_ref, dtb_ref, c_ref, sa_ref, sb_ref,
                    o_ref, gb_ref, vt_ref, h_ref):
    j = pl.program_id(2)

    @pl.when(j == 0)
    def _():
        n_rows = x_ref.shape[0]
        blk = min(IN_NORM_ROWS, n_rows)
        for r0 in range(0, n_rows, blk):
            rows = slice(r0, r0 + blk)
            x = x_ref[rows, :]
            ms = jnp.mean(x * x, axis=-1, keepdims=True)
            h = (x * lax.rsqrt(ms + EPS) * nw_ref[...]).astype(BF16)
            h_ref[rows, :] = h
            raw = jnp.dot(h, wgb_ref[...], preferred_element_type=F32)
            lane = lax.broadcasted_iota(jnp.int32, raw.shape, 1)
            g = -jnp.exp(alog_ref[...]) * _softplus(raw + dtb_ref[...])
            is_beta = (lane >= 8) & (lane < 16)
            gb_ref[rows, :] = jnp.where(is_beta, _sigmoid(raw), g)
            o_ref[rows, :] = jnp.dot(h, w_ref[...], preferred_element_type=F32).astype(o_ref.dtype)

    def project():
        return jnp.dot(h_ref[...], w_ref[...], preferred_element_type=F32)

    @pl.when(j == IN_TILE_QK)
    def _():
        acc = project()
        c = c_ref[...]
        sa = sa_ref[...]
        sb = sb_ref[...]
        half = ROT_DIM // 2
        q_scale = DQK ** -0.5 * math.log2(math.e)
        for hh in range(2 * N_HEADS):
            sl = slice(hh * D_HEAD, (hh + 1) * D_HEAD)
            t = acc[:, sl]
            y = t * c + pltpu.roll(t, half, 1) * sa + pltpu.roll(t, D_HEAD - half, 1) * sb
            if hh < N_HEADS:
                y = y * q_scale
            o_ref[:, sl] = y.astype(o_ref.dtype)

    @pl.when(j == IN_TILE_VZ)
    def _():
        acc = project()
        o_ref[...] = acc.astype(o_ref.dtype)
        ones = jnp.ones((VT_ROWS - D_HEAD, acc.shape[0]), vt_ref.dtype)
        for hh in range(N_HEADS):
            vt_ref[hh, :D_HEAD, :] = acc[:, hh * D_HEAD:(hh + 1) * D_HEAD].T.astype(vt_ref.dtype)
            vt_ref[hh, D_HEAD:, :] = ones

    @pl.when((j != 0) & (j != IN_TILE_QK) & (j != IN_TILE_VZ))
    def _():
        o_ref[...] = project().astype(o_ref.dtype)


def _in_proj(x, nw, w_main, w_gb, alog, dtb, cos_t, sin_a, sin_b):
    B, S, D = x.shape
    tm = min(1024, S)
    tn = IN_TILE_COLS
    tab = pl.BlockSpec((tm, D_HEAD), lambda b, i, j: (i, 0))
    return pl.pallas_call(
        _in_proj_kernel,
        out_shape=(jax.ShapeDtypeStruct((B, S, N_MAIN), BF16),
                   jax.ShapeDtypeStruct((B, S, LANES), F32),
                   jax.ShapeDtypeStruct((B, N_HEADS, VT_ROWS, S), BF16)),
        grid=(B, S // tm, N_MAIN // tn),
        in_specs=[
            pl.BlockSpec((None, tm, D), lambda b, i, j: (b, i, 0)),
            pl.BlockSpec((1, D), lambda b, i, j: (0, 0)),
            pl.BlockSpec((D, tn), lambda b, i, j: (0, j)),
            pl.BlockSpec((D, LANES), lambda b, i, j: (0, 0)),
            pl.BlockSpec((1, LANES), lambda b, i, j: (0, 0)),
            pl.BlockSpec((1, LANES), lambda b, i, j: (0, 0)),
            tab, tab, tab,
        ],
        out_specs=(pl.BlockSpec((None, tm, tn), lambda b, i, j: (b, i, j)),
                   pl.BlockSpec((None, tm, LANES), lambda b, i, j: (b, i, 0)),
                   pl.BlockSpec((None, N_HEADS, VT_ROWS, tm), lambda b, i, j: (b, 0, 0, i))),
        scratch_shapes=[pltpu.VMEM((tm, D), BF16)],
        compiler_params=_cparams("parallel", "parallel", "arbitrary"),
        name="in_proj",
    )(x, nw, w_main, w_gb, alog, dtb, cos_t, sin_a, sin_b)


def _kv_proj_kernel(x_ref, nw_ref, w_ref, o_ref):
    x = x_ref[...]
    ms = jnp.mean(x * x, axis=-1, keepdims=True)
    h = (x * lax.rsqrt(ms + EPS) * nw_ref[...]).astype(BF16)
    kv = jnp.dot(h, w_ref[...], preferred_element_type=F32)
    n_k = kv.shape[1] // 2
    o_ref[:, :n_k] = (kv[:, :n_k] * (D_HEAD ** -0.5 * math.log2(math.e))).astype(o_ref.dtype)
    o_ref[:, n_k:] = kv[:, n_k:].astype(o_ref.dtype)


def _kv_proj(mem, nw, w_kv):
    B, M, D = mem.shape
    n_out = w_kv.shape[1]
    return pl.pallas_call(
        _kv_proj_kernel,
        out_shape=jax.ShapeDtypeStruct((B, M, n_out), BF16),
        grid=(B,),
        in_specs=[
            pl.BlockSpec((None, M, D), lambda b: (b, 0, 0)),
            pl.BlockSpec((1, D), lambda b: (0, 0)),
            pl.BlockSpec((D, n_out), lambda b: (0, 0)),
        ],
        out_specs=pl.BlockSpec((None, M, n_out), lambda b: (b, 0, 0)),
        compiler_params=_cparams("parallel"),
        name="kv_proj",
    )(mem, nw, w_kv)


def _gdn_prep_kernel(xp_ref, x_ref, xn_ref, gb_ref, cw_ref, cm_ref, sh_ref, qkv_ref, col_ref):
    i = pl.program_id(1)
    n_i = pl.num_programs(1)
    T = x_ref.shape[0]
    pad = GDN_CONV // 2
    prev = jnp.where(i > 0, xp_ref[...], jnp.zeros_like(xp_ref[...]))
    nxt = jnp.where(i < n_i - 1, xn_ref[...], jnp.zeros_like(xn_ref[...]))
    xe = jnp.concatenate([prev, x_ref[...], nxt], axis=0)
    blk = min(CONV_ROWS, T)
    for r0 in range(0, T, blk):
        win = xe[r0:r0 + blk + 2 * HALO, :]
        taps = jnp.dot(sh_ref[...], win, preferred_element_type=F32)
        acc = xe[r0 + HALO:r0 + HALO + blk, :].astype(F32) * cw_ref[pad:pad + 1, :]
        for n, d in enumerate(dd for dd in range(GDN_CONV) if dd != pad):
            acc = acc + taps[n * blk:(n + 1) * blk, :] * cw_ref[d:d + 1, :]
        y = _silu(acc)
        rows = slice(r0, r0 + blk)
        for h in range(2 * N_HEADS):
            sl = slice(h * D_HEAD, (h + 1) * D_HEAD)
            t = y[:, sl]
            r = lax.rsqrt(jnp.sum(t * t, axis=-1, keepdims=True) + EPS)
            if h < N_HEADS:
                r = r * (D_HEAD ** -0.5)
            qkv_ref[rows, sl] = t * r
        qkv_ref[rows, 2 * W_BR:] = y[:, 2 * W_BR:]

        gb = gb_ref[rows, :]
        lane = lax.broadcasted_iota(jnp.int32, gb.shape, 1)
        hi, mid, lo = (p.astype(F32) for p in _split3(jnp.where(lane < GB_LANES, gb, 0.0)))
        packed = hi + pltpu.roll(mid, GB_LANES, 1) + pltpu.roll(lo, 2 * GB_LANES, 1)
        sums = jnp.dot(cm_ref[...], packed.astype(BF16), preferred_element_type=F32)
        sums = (sums + pltpu.roll(sums, LANES - GB_LANES, 1)
                + pltpu.roll(sums, LANES - 2 * GB_LANES, 1))
        col_ref[rows, :] = jnp.where(lane < 4, sums[:blk],
                                     jnp.where(lane < 8, sums[blk:2 * blk],
                                               jnp.where(lane < 16, gb, sums[2 * blk:])))


def _chunk_sum_matrices(T):
    r_i = np.arange(T)[:, None]
    c_i = np.arange(T)[None, :]
    same = (r_i // GDN_CHUNK) == (c_i // GDN_CHUNK)
    mats = np.concatenate([same & (c_i <= r_i), same & (c_i >= r_i), same], axis=0)
    return jnp.asarray(mats.astype(np.float32), dtype=BF16)


def _conv_shift_matrix(blk):
    pad = GDN_CONV // 2
    r_i = np.arange(blk)[:, None]
    c_i = np.arange(blk + 2 * HALO)[None, :]
    mats = np.concatenate([c_i == r_i + HALO + d - pad for d in range(GDN_CONV) if d != pad], axis=0)
    return jnp.asarray(mats.astype(np.float32), dtype=BF16)


def _gdn_prep(proj, gb, conv_w):
    B, S, _ = proj.shape
    T = min(1024, S)
    nb = T // HALO
    n_halo = S // HALO
    wq = 3 * W_BR
    blk = min(CONV_ROWS, T)
    return pl.pallas_call(
        _gdn_prep_kernel,
        out_shape=(jax.ShapeDtypeStruct((B, S, wq), F32),
                   jax.ShapeDtypeStruct((B, S, LANES), F32)),
        grid=(B, S // T),
        in_specs=[
            pl.BlockSpec((None, HALO, wq), lambda b, i: (b, jnp.maximum(i * nb - 1, 0), 0)),
            pl.BlockSpec((None, T, wq), lambda b, i: (b, i, 0)),
            pl.BlockSpec((None, HALO, wq), lambda b, i: (b, jnp.minimum((i + 1) * nb, n_halo - 1), 0)),
            pl.BlockSpec((None, T, LANES), lambda b, i: (b, i, 0)),
            pl.BlockSpec((GDN_CONV, wq), lambda b, i: (0, 0)),
            pl.BlockSpec((3 * blk, blk), lambda b, i: (0, 0)),
            pl.BlockSpec(((GDN_CONV - 1) * blk, blk + 2 * HALO), lambda b, i: (0, 0)),
        ],
        out_specs=(pl.BlockSpec((None, T, wq), lambda b, i: (b, i, 0)),
                   pl.BlockSpec((None, T, LANES), lambda b, i: (b, i, 0))),
        compiler_params=_cparams("parallel", "parallel"),
        name="gdn_prep",
    )(proj, proj, proj, gb, conv_w, _chunk_sum_matrices(blk), _conv_shift_matrix(blk))


def _gdn_masks(reverse):
    n = N_HEADS * GDN_CHUNK
    r_i = lax.broadcasted_iota(jnp.int32, (n, n), 0)
    c_i = lax.broadcasted_iota(jnp.int32, (n, n), 1)
    same_head = _same_block(r_i, c_i, GDN_CHUNK)
    same_blk = _same_block(r_i, c_i, 8)
    if reverse:
        incl = same_head & (c_i >= r_i)
        strict = same_head & (c_i > r_i)
    else:
        incl = same_head & (c_i <= r_i)
        strict = same_head & (c_i < r_i)
    return incl, strict, same_blk


def _stack_heads(x):
    return jnp.concatenate([x[:, h * D_HEAD:(h + 1) * D_HEAD] for h in range(N_HEADS)], axis=0)


def _mm(a, b):
    return jnp.dot(a, b, preferred_element_type=F32)


def _bf(xs):
    return [x.astype(BF16) for x in xs]


def _unit_tri_solve(a_mats, rhss, same_blks):
    n = a_mats[0].shape[0]
    eye = jnp.where(lax.broadcasted_iota(jnp.int32, (n, n), 0)
                    == lax.broadcasted_iota(jnp.int32, (n, n), 1), 1.0, 0.0)
    d1 = [jnp.where(m, a, 0.0) for a, m in zip(a_mats, same_blks)]
    lo_rhs = _bf([jnp.concatenate([a - d, r], axis=1) for a, d, r in zip(a_mats, d1, rhss)])
    d1b = _bf(d1)
    d2 = [_mm(d, d) for d in d1b]
    d2b = _bf(d2)
    d4b = _bf([_mm(d, d) for d in d2b])
    d3 = [_mm(a, b) for a, b in zip(d1b, d2b)]
    p1 = [eye - a + b - c for a, b, c in zip(d1, d2, d3)]
    t8b = _bf([p + _mm(p.astype(BF16), d) for p, d in zip(p1, d4b)])
    by = [_mm(t, w) for t, w in zip(t8b, lo_rhs)]
    b1b = _bf([x[:, :n] for x in by])
    b2b = _bf([_mm(b, b) for b in b1b])
    b4b = _bf([_mm(b, b) for b in b2b])
    z = [x[:, n:] for x in by]
    for bb in (b1b, b2b, b4b):
        sign = -1.0 if bb is b1b else 1.0
        z = [a + sign * _mm(b, a.astype(BF16)) for a, b in zip(z, bb)]
    return z


def _gdn_prepare(q, k, v, col, row, d):
    g_idx = d * N_HEADS
    b_idx = 8 + d * N_HEADS
    t_idx = 16 + d * N_HEADS
    qs, kbs, ks, rhs_v, rhs_k, gcs, kds, e_tot = [], [], [], [], [], [], [], []
    for h in range(N_HEADS):
        sl = slice(h * D_HEAD, (h + 1) * D_HEAD)
        gc = col[:, g_idx + h:g_idx + h + 1]
        bc = col[:, b_idx + h:b_idx + h + 1]
        gt = col[:, t_idx + h:t_idx + h + 1]
        eg = jnp.exp(gc)
        k_h = k[:, sl]
        kb = k_h * bc
        qs.append(q[:, sl] * eg)
        kbs.append(kb)
        ks.append(k_h)
        rhs_v.append(v[:, sl] * bc)
        rhs_k.append(kb * eg)
        gcs.append(jnp.broadcast_to(gc, (GDN_CHUNK, N_HEADS * GDN_CHUNK)))
        kds.append((k_h * jnp.exp(gt - gc)).astype(BF16))
        e_tot.append(jnp.exp(row[8 + g_idx + h:8 + g_idx + h + 1, :D_HEAD]))
    return dict(
        gram_lhs=jnp.concatenate(kbs + [_stack_heads(q)], axis=0).astype(BF16),
        k_st=jnp.concatenate(ks, axis=0).astype(BF16),
        dlog=jnp.concatenate(gcs, axis=0) - row[d:d + 1, :],
        rhs=jnp.concatenate([jnp.concatenate(rhs_v, axis=0), jnp.concatenate(rhs_k, axis=0)], axis=1),
        qs=jnp.concatenate(qs, axis=0), kds=kds, e_tot=e_tot)


def _gdn_affine_terms(chains, masks):
    n = N_HEADS * GDN_CHUNK
    gram = [lax.dot_general(c["gram_lhs"], c["k_st"], (((1,), (1,)), ((), ())),
                            preferred_element_type=F32) for c in chains]
    decay = [jnp.exp(jnp.where(m[0], c["dlog"], NEG)) for c, m in zip(chains, masks)]
    a_mats = [jnp.where(m[1], g[:n] * dc, 0.0) for g, dc, m in zip(gram, decay, masks)]
    attn_b = _bf([g[n:] * dc for g, dc in zip(gram, decay)])
    xb = _bf(_unit_tri_solve(a_mats, [c["rhs"] for c in chains], [m[2] for m in masks]))
    ax = [_mm(a, x) for a, x in zip(attn_b, xb)]
    out = []
    for c, x, a in zip(chains, xb, ax):
        heads = []
        for h in range(N_HEADS):
            rs = slice(h * GDN_CHUNK, (h + 1) * GDN_CHUNK)
            kx = lax.dot_general(c["kds"][h], x[rs], (((0,), (0,)), ((), ())),
                                 preferred_element_type=F32)
            p = c["qs"][rs] - a[rs, D_HEAD:]
            heads.append(dict(lhs=jnp.concatenate([kx[:, D_HEAD:], p], axis=0).astype(BF16),
                              n=kx[:, :D_HEAD], r=a[rs, :D_HEAD], e=c["e_tot"][h]))
        out.append(heads)
    return out


def _gdn_kernel(qf_ref, kf_ref, vf_ref, colf_ref, rowf_ref,
                qb_ref, kb_ref, vb_ref, colb_ref, rowb_ref,
                of_ref, ob_ref, s_ref):
    nc = qf_ref.shape[0]

    @pl.when(pl.program_id(1) == 0)
    def _():
        s_ref[...] = jnp.zeros(s_ref.shape, F32)

    masks_f = _gdn_masks(False)
    masks_b = _gdn_masks(True)
    chains, masks = [], []
    for c in range(nc):
        cb = nc - 1 - c
        chains.append(_gdn_prepare(qf_ref[c], kf_ref[c], vf_ref[c], colf_ref[c], rowf_ref[c], 0))
        chains.append(_gdn_prepare(qb_ref[cb], kb_ref[cb], vb_ref[cb], colb_ref[cb], rowb_ref[cb], 1))
        masks += [masks_f, masks_b]
    terms = _gdn_affine_terms(chains, masks)

    state = [[s_ref[d, h] for h in range(N_HEADS)] for d in range(2)]
    for c in range(nc):
        ls = [[_mm(terms[2 * c + d][h]["lhs"], state[d][h].astype(BF16)) for h in range(N_HEADS)]
              for d in range(2)]
        for d, o_ref, idx in ((0, of_ref, c), (1, ob_ref, nc - 1 - c)):
            outs = []
            for h in range(N_HEADS):
                t = terms[2 * c + d][h]
                state[d][h] = state[d][h] * t["e"] - ls[d][h][:D_HEAD] + t["n"]
                outs.append(ls[d][h][D_HEAD:] + t["r"])
            o_ref[idx] = jnp.concatenate(outs, axis=1)
    for d in range(2):
        for h in range(N_HEADS):
            s_ref[d, h] = state[d][h]


def _gdn(qkv, col, row):
    B, N, C, _ = qkv.shape
    nc = min(8, N)
    nblk = N // nc
    fwd = lambda j: (lambda b, i: (b, i, 0, j))
    bwd = lambda j: (lambda b, i: (b, nblk - 1 - i, 0, j))
    specs = []
    for mk in (fwd, bwd):
        specs += [pl.BlockSpec((None, nc, C, W_BR), mk(0)),
                  pl.BlockSpec((None, nc, C, W_BR), mk(1)),
                  pl.BlockSpec((None, nc, C, W_BR), mk(2)),
                  pl.BlockSpec((None, nc, C, LANES), mk(0)),
                  pl.BlockSpec((None, nc, 16, 2 * LANES), mk(0))]
    out_sds = jax.ShapeDtypeStruct((B, N, C, W_BR), F32)
    return pl.pallas_call(
        _gdn_kernel,
        out_shape=(out_sds, out_sds),
        grid=(B, nblk),
        in_specs=specs,
        out_specs=(pl.BlockSpec((None, nc, C, W_BR), fwd(0)),
                   pl.BlockSpec((None, nc, C, W_BR), bwd(0))),
        scratch_shapes=[pltpu.VMEM((2, N_HEADS, D_HEAD, D_HEAD), F32)],
        compiler_params=_cparams("parallel", "arbitrary"),
        name="gdn",
    )(qkv, qkv, qkv, col, row, qkv, qkv, qkv, col, row)


def _diff_attn_kernel(q_ref, k_ref, vt_ref, z_ref, lp_ref, nw_ref, y_ref, *, lambda_init):
    tq = q_ref.shape[0]
    sub = min(DIFF_SUB, tq)
    n_keys = k_ref.shape[0]
    ck = min(DIFF_KEYS, n_keys)
    lane = lax.broadcasted_iota(jnp.int32, (sub, D_HEAD), 1)
    lp = lp_ref[...]
    lam = (jnp.exp(jnp.sum(lp[0:1] * lp[1:2], axis=-1, keepdims=True))
           - jnp.exp(jnp.sum(lp[2:3] * lp[3:4], axis=-1, keepdims=True)) + lambda_init)
    streams = [(j, first) for j in range(tq // sub) for first in (True, False)]
    q_masked = []
    for j, first in streams:
        q = q_ref[j * sub:(j + 1) * sub, :]
        q_masked.append(jnp.where((lane < DQK) == first, q, jnp.zeros_like(q)))

    def scores(c, t):
        k_c = k_ref[c * ck:(c + 1) * ck, :]
        return lax.dot_general(k_c, q_masked[t], (((1,), (1,)), ((), ())),
                               preferred_element_type=F32)

    run_max = [None] * len(streams)
    acc = [None] * len(streams)
    pending = [scores(0, t) for t in range(len(streams))]
    for c in range(n_keys // ck):
        vt_c = vt_ref[:, c * ck:(c + 1) * ck]
        for t in range(len(streams)):
            s = pending[t]
            if (c + 1) * ck < n_keys:
                pending[t] = scores(c + 1, t)
            m_c = jnp.max(s, axis=0, keepdims=True)
            m_new = m_c if c == 0 else jnp.maximum(run_max[t], m_c)
            p = jnp.exp2(s - m_new).astype(BF16)
            pv = jnp.dot(vt_c, p, preferred_element_type=F32)
            acc[t] = pv if c == 0 else acc[t] * jnp.exp2(run_max[t] - m_new) + pv
            run_max[t] = m_new

    for j in range(tq // sub):
        halves = []
        for first in (True, False):
            o_aug = acc[streams.index((j, first))]
            inv = 1.0 / o_aug[D_HEAD:D_HEAD + 1, :]
            halves.append(o_aug[:D_HEAD, :] * (inv if first else inv * lam))
        o = (halves[0] - halves[1]).T
        r = lax.rsqrt(jnp.mean(o * o, axis=-1, keepdims=True) + EPS)
        o = o * r * nw_ref[...] * (1.0 - lambda_init)
        rows = slice(j * sub, (j + 1) * sub)
        y_ref[rows, :] = (o * _silu(z_ref[rows, :].astype(F32))).astype(y_ref.dtype)


def _diff_attn(proj, v_t, lam_p, nw, lambda_init):
    B, S, _ = proj.shape
    tq = min(8 * DIFF_SUB, S)
    nh = W_BR // D_HEAD
    return pl.pallas_call(
        functools.partial(_diff_attn_kernel, lambda_init=lambda_init),
        out_shape=jax.ShapeDtypeStruct((B, S, W_BR), BF16),
        grid=(B, N_HEADS, S // tq),
        in_specs=[
            pl.BlockSpec((None, tq, D_HEAD), lambda b, h, i: (b, i, COL_B_Q * nh + h)),
            pl.BlockSpec((None, S, D_HEAD), lambda b, h, i: (b, 0, COL_B_K * nh + h)),
            pl.BlockSpec((None, None, VT_ROWS, S), lambda b, h, i: (b, h, 0, 0)),
            pl.BlockSpec((None, tq, D_HEAD), lambda b, h, i: (b, i, COL_B_Z * nh + h)),
            pl.BlockSpec((4, DQK), lambda b, h, i: (0, 0)),
            pl.BlockSpec((1, D_HEAD), lambda b, h, i: (0, 0)),
        ],
        out_specs=pl.BlockSpec((None, tq, D_HEAD), lambda b, h, i: (b, i, h)),
        compiler_params=_cparams("parallel", "parallel", "parallel"),
        name="diff_attn",
    )(proj, proj, v_t, proj, lam_p, nw)


def _conv_mem_kernel(cb_ref, cc_ref, cx_ref, cz_ref, ccp_ref, cxp_ref, ccn_ref, cxn_ref,
                     mq_ref, mz_ref, kv_ref, cw_ref, y_ref):
    i = pl.program_id(1)
    n_i = pl.num_programs(1)
    T = cb_ref.shape[0]
    prev = jnp.where(i > 0, ccp_ref[HALO - 8:, :].astype(F32) * cxp_ref[HALO - 8:, :].astype(F32), 0.0)
    nxt = jnp.where(i < n_i - 1, ccn_ref[:8, :].astype(F32) * cxn_ref[:8, :].astype(F32), 0.0)
    cur = cc_ref[...].astype(F32) * cx_ref[...].astype(F32)
    xe = jnp.concatenate([prev, cur, nxt], axis=0)
    pad = CONV_W // 2
    acc = xe[8 - pad:8 - pad + T, :] * cw_ref[0:1, :]
    for d in range(1, CONV_W):
        acc = acc + xe[8 - pad + d:8 - pad + d + T, :] * cw_ref[d:d + 1, :]
    y_c = cb_ref[...].astype(F32) * acc * _silu(cz_ref[...].astype(F32))
    y_ref[:, :W_BR] = y_c.astype(y_ref.dtype)

    mz = mz_ref[...].astype(F32)
    ones = jnp.ones((kv_ref.shape[0], D_HEAD), kv_ref.dtype)
    for h in range(N_HEADS):
        sl = slice(h * D_HEAD, (h + 1) * D_HEAD)
        s = lax.dot_general(mq_ref[:, sl], kv_ref[:, sl], (((1,), (1,)), ((), ())),
                            preferred_element_type=F32)
        p = jnp.exp2(s - jnp.max(s, axis=-1, keepdims=True)).astype(BF16)
        v_aug = jnp.concatenate([kv_ref[:, W_BR + h * D_HEAD:W_BR + (h + 1) * D_HEAD], ones], axis=1)
        o = jnp.dot(p, v_aug, preferred_element_type=F32)
        o = o[:, :D_HEAD] * (1.0 / o[:, D_HEAD:])
        y_ref[:, W_BR + h * D_HEAD:W_BR + (h + 1) * D_HEAD] = (o * _silu(mz[:, sl])).astype(y_ref.dtype)


def _conv_mem(proj, kv, conv_w):
    B, S, _ = proj.shape
    M = kv.shape[1]
    T = min(1024, S)
    nb = T // HALO
    n_halo = S // HALO
    main = lambda c: pl.BlockSpec((None, T, W_BR), lambda b, i: (b, i, c))
    prev = lambda c: pl.BlockSpec((None, HALO, W_BR), lambda b, i: (b, jnp.maximum(i * nb - 1, 0), c))
    nxt = lambda c: pl.BlockSpec((None, HALO, W_BR),
                                 lambda b, i: (b, jnp.minimum((i + 1) * nb, n_halo - 1), c))
    return pl.pallas_call(
        _conv_mem_kernel,
        out_shape=jax.ShapeDtypeStruct((B, S, 2 * W_BR), BF16),
        grid=(B, S // T),
        in_specs=[main(COL_C_B), main(COL_C_C), main(COL_C_X), main(COL_C_Z),
                  prev(COL_C_C), prev(COL_C_X), nxt(COL_C_C), nxt(COL_C_X),
                  main(COL_M_Q), main(COL_M_Z),
                  pl.BlockSpec((None, M, 2 * W_BR), lambda b, i: (b, 0, 0)),
                  pl.BlockSpec((CONV_W, W_BR), lambda b, i: (0, 0))],
        out_specs=pl.BlockSpec((None, T, 2 * W_BR), lambda b, i: (b, i, 0)),
        compiler_params=_cparams("parallel", "parallel"),
        name="conv_mem",
    )(proj, proj, proj, proj, proj, proj, proj, proj, proj, proj, kv, conv_w)


def _out_proj_kernel(of_ref, ob_ref, az_ref, gnw_ref, yb_ref, ycm_ref, w_ref, x_ref, nw_ref, o_ref):
    o = of_ref[...] + ob_ref[...]
    z = az_ref[...].astype(F32)
    heads = []
    for h in range(N_HEADS):
        sl = slice(h * D_HEAD, (h + 1) * D_HEAD)
        t = o[:, sl]
        r = lax.rsqrt(jnp.mean(t * t, axis=-1, keepdims=True) + EPS)
        heads.append((t * r * gnw_ref[...] * _silu(z[:, sl])).astype(BF16))
    y = jnp.dot(jnp.concatenate([yb_ref[...], ycm_ref[...]] + heads, axis=1), w_ref[...],
                preferred_element_type=F32)
    r = lax.rsqrt(jnp.mean(y * y, axis=-1, keepdims=True) + EPS)
    o_ref[...] = x_ref[...] + y * r * nw_ref[...]


def _out_proj(o_f, o_b, proj, gdn_nw, y_b, y_cm, w_out, x, nw):
    B, S, D = x.shape
    T = min(512, S)
    return pl.pallas_call(
        _out_proj_kernel,
        out_shape=jax.ShapeDtypeStruct((B, S, D), F32),
        grid=(B, S // T),
        in_specs=[
            pl.BlockSpec((None, T, W_BR), lambda b, i: (b, i, 0)),
            pl.BlockSpec((None, T, W_BR), lambda b, i: (b, i, 0)),
            pl.BlockSpec((None, T, W_BR), lambda b, i: (b, i, COL_A_Z)),
            pl.BlockSpec((1, D_HEAD), lambda b, i: (0, 0)),
            pl.BlockSpec((None, T, W_BR), lambda b, i: (b, i, 0)),
            pl.BlockSpec((None, T, 2 * W_BR), lambda b, i: (b, i, 0)),
            pl.BlockSpec((4 * W_BR, D), lambda b, i: (0, 0)),
            pl.BlockSpec((None, T, D), lambda b, i: (b, i, 0)),
            pl.BlockSpec((1, D), lambda b, i: (0, 0)),
        ],
        out_specs=pl.BlockSpec((None, T, D), lambda b, i: (b, i, 0)),
        compiler_params=_cparams("parallel", "parallel"),
        name="out_proj",
    )(o_f, o_b, proj, gdn_nw, y_b, y_cm, w_out, x, nw)


def _rope_tables(S):
    half = ROT_DIM // 2
    f32 = np.float32
    inv = (f32(ROPE_THETA) ** (-np.arange(0, ROT_DIM, 2, dtype=f32) / f32(ROT_DIM))).astype(f32)
    ang = np.arange(S, dtype=f32)[:, None] * inv[None, :]
    cos, sin = np.cos(ang).astype(f32), np.sin(ang).astype(f32)
    ones = np.ones((S, DQK - ROT_DIM), f32)
    zeros = np.zeros((S, DQK - ROT_DIM), f32)
    z8 = np.zeros((S, half), f32)
    c_map = np.concatenate([cos, cos, ones], axis=1)
    sa_map = np.concatenate([z8, sin, zeros], axis=1)
    sb_map = np.concatenate([-sin, z8, zeros], axis=1)
    two = lambda t: jnp.asarray(np.concatenate([t, t], axis=1))
    return two(c_map), two(sa_map), two(sb_map)


def _row_info(col, N):
    B = col.shape[0]
    c4 = col.reshape(B, N, GDN_CHUNK, LANES)
    g_st = jnp.swapaxes(c4[..., 0:8], 2, 3).reshape(B, N, 2, N_HEADS * GDN_CHUNK)
    tot = jnp.broadcast_to(c4[:, :, 0, 16:24][..., None], (B, N, 8, N_HEADS * GDN_CHUNK))
    pad = jnp.zeros((B, N, 6, N_HEADS * GDN_CHUNK), F32)
    return jnp.concatenate([g_st, pad, tot], axis=2)


def _prep_layer_weights(l, norm_pre, norm_post, norm_mem, w_in, gdn_conv, gdn_A_log, gdn_dt_bias,
                        gdn_norm, diff_lambda, diff_norm, conv_w, w_mem_kv, w_out):
    w = w_in[l]
    q_end = 3 * W_BR
    n_db = 2 * N_HEADS
    dec = w[:, q_end:q_end + n_db]
    bet = w[:, q_end + n_db:q_end + 2 * n_db]
    w_main = jnp.concatenate([w[:, :q_end], w[:, q_end + 2 * n_db:]], axis=1).astype(BF16)
    w_gb = jnp.concatenate([dec, bet, dec, jnp.zeros((w.shape[0], LANES - 3 * n_db), F32)],
                           axis=1).astype(BF16)
    lane_pad = lambda t: jnp.concatenate(
        [t.reshape(1, n_db), jnp.zeros((1, n_db), F32), t.reshape(1, n_db),
         jnp.zeros((1, LANES - 3 * n_db), F32)], axis=1)
    return dict(
        norm_pre=norm_pre[l][None, :], norm_post=norm_post[l][None, :], norm_mem=norm_mem[l][None, :],
        w_main=w_main, w_gb=w_gb, alog=lane_pad(gdn_A_log[l]), dtb=lane_pad(gdn_dt_bias[l]),
        gdn_conv=gdn_conv[l], gdn_norm=gdn_norm[l][None, :], diff_lambda=diff_lambda[l],
        diff_norm=diff_norm[l][None, :], conv_w=conv_w[l], w_kv=w_mem_kv[l].astype(BF16),
        w_out=jnp.concatenate([w_out[l][W_BR:], w_out[l][:W_BR]], axis=0).astype(BF16))


def _layer(x, mem, p, lambda_init, tables):
    B, S, _ = x.shape
    N = S // GDN_CHUNK
    proj, gb, v_t = _in_proj(x, p["norm_pre"], p["w_main"], p["w_gb"], p["alog"], p["dtb"], *tables)
    kv = _kv_proj(mem, p["norm_mem"], p["w_kv"])
    qkv, col = _gdn_prep(proj, gb, p["gdn_conv"])
    row = _row_info(col, N)
    o_f, o_b = _gdn(qkv.reshape(B, N, GDN_CHUNK, 3 * W_BR), col.reshape(B, N, GDN_CHUNK, LANES), row)
    y_b = _diff_attn(proj, v_t, p["diff_lambda"], p["diff_norm"], lambda_init)
    y_cm = _conv_mem(proj, kv, p["conv_w"])
    return _out_proj(o_f.reshape(B, S, W_BR), o_b.reshape(B, S, W_BR), proj, p["gdn_norm"],
                     y_b, y_cm, p["w_out"], x, p["norm_post"])


def _trunk(x, mem, layer_params):
    tables = _rope_tables(x.shape[1])
    for l, p in enumerate(layer_params):
        lambda_init = 0.8 - 0.6 * math.exp(-0.3 * l)
        x = _layer(x, mem, p, lambda_init, tables)
    return x


def kernel(x_prompt, x_sample, mem_prompt, mem_sample, norm_pre, norm_post, norm_mem, w_in, gdn_conv,
           gdn_A_log, gdn_dt_bias, gdn_norm, diff_lambda, diff_norm, conv_w, w_mem_kv, w_out):
    depth = w_in.shape[0]
    params = [_prep_layer_weights(l, norm_pre, norm_post, norm_mem, w_in, gdn_conv, gdn_A_log,
                                  gdn_dt_bias, gdn_norm, diff_lambda, diff_norm, conv_w, w_mem_kv, w_out)
              for l in range(depth)]
    return (_trunk(x_prompt, mem_prompt, params), _trunk(x_sample, mem_sample, params))
```

```python
import functools
import math

import jax
import jax.numpy as jnp
import numpy as np
from jax import lax
from jax.experimental import pallas as pl
from jax.experimental.pallas import tpu as pltpu

F32 = jnp.float32
BF16 = jnp.bfloat16
EPS = 1e-6

W_BR = 512
N_HEADS = 4
D_HEAD = 128
DQK = 64
ROT_DIM = 16
ROPE_THETA = 500000.0
GDN_CONV = 5
GDN_CHUNK = 64
CONV_W = 3
N_MAIN = 14 * W_BR
LANES = 128
HALO = 16
VT_ROWS = D_HEAD + 16
GB_LANES = 32
CONV_ROWS = 128
DIFF_SUB = 256
DIFF_KEYS = 512
NEG = -1e30
VMEM_LIMIT = 56 * 1024 * 1024

COL_A_Z = 3
COL_B_Q, COL_B_K, COL_B_V, COL_B_Z = 4, 5, 6, 7
COL_C_B, COL_C_C, COL_C_X, COL_C_Z = 8, 9, 10, 11
COL_M_Q, COL_M_Z = 12, 13
IN_TILE_COLS = 2 * W_BR
IN_NORM_ROWS = 256
IN_TILE_QK = COL_B_Q // 2
IN_TILE_VZ = COL_B_V // 2


def _cparams(*sem):
    return pltpu.CompilerParams(dimension_semantics=sem, vmem_limit_bytes=VMEM_LIMIT)


def _sigmoid(x):
    return 1.0 / (1.0 + jnp.exp2(x * -math.log2(math.e)))


def _silu(x):
    return x * _sigmoid(x)


def _softplus(x):
    return jnp.maximum(x, 0.0) + jnp.log(1.0 + jnp.exp(-jnp.abs(x)))


def _same_block(r_i, c_i, size):
    shift = size.bit_length() - 1
    return jnp.right_shift(r_i, shift) == jnp.right_shift(c_i, shift)


def _split3(x):
    hi = x.astype(BF16)
    r1 = x - hi.astype(F32)
    mid = r1.astype(BF16)
    lo = (r1 - mid.astype(F32)).astype(BF16)
    return hi, mid, lo


def _in_proj_kernel(x_ref, nw_ref, w_ref, wgb_ref, alog_ref, dtb_ref, c_ref, sa_ref, sb_ref,
                    o_ref, gb_ref, vt_ref, h_ref):
    j = pl.program_id(2)

    @pl.when(j == 0)
    def _():
        n_rows = x_ref.shape[0]
        blk = min(IN_NORM_ROWS, n_rows)
        for r0 in range(0, n_rows, blk):
            rows = slice(r0, r0 + blk)
            x = x_ref[rows, :]
            ms = jnp.mean(x * x, axis=-1, keepdims=True)
            h = (x * lax.rsqrt(ms + EPS) * nw_ref[...]).astype(BF16)
            h_ref[rows, :] = h
            raw = jnp.dot(h, wgb_ref[...], preferred_element_type=F32)
            lane = lax.broadcasted_iota(jnp.int32, raw.shape, 1)
            g = -jnp.exp(alog_ref[...]) * _softplus(raw + dtb_ref[...])
            is_beta = (lane >= 8) & (lane < 16)
            gb_ref[rows, :] = jnp.where(is_beta, _sigmoid(raw), g)
            o_ref[rows, :] = jnp.dot(h, w_ref[...], preferred_element_type=F32).astype(o_ref.dtype)

    def project():
        return jnp.dot(h_ref[...], w_ref[...], preferred_element_type=F32)

    @pl.when(j == IN_TILE_QK)
    def _():
        acc = project()
        c = c_ref[...]
        sa = sa_ref[...]
        sb = sb_ref[...]
        half = ROT_DIM // 2
        q_scale = DQK ** -0.5 * math.log2(math.e)
        for hh in range(2 * N_HEADS):
            sl = slice(hh * D_HEAD, (hh + 1) * D_HEAD)
            t = acc[:, sl]
            y = t * c + pltpu.roll(t, half, 1) * sa + pltpu.roll(t, D_HEAD - half, 1) * sb
            if hh < N_HEADS:
                y = y * q_scale
            o_ref[:, sl] = y.astype(o_ref.dtype)

    @pl.when(j == IN_TILE_VZ)
    def _():
        acc = project()
        o_ref[...] = acc.astype(o_ref.dtype)
        ones = jnp.ones((VT_ROWS - D_HEAD, acc.shape[0]), vt_ref.dtype)
        for hh in range(N_HEADS):
            vt_ref[hh, :D_HEAD, :] = acc[:, hh * D_HEAD:(hh + 1) * D_HEAD].T.astype(vt_ref.dtype)
            vt_ref[hh, D_HEAD:, :] = ones

    @pl.when((j != 0) & (j != IN_TILE_QK) & (j != IN_TILE_VZ))
    def _():
        o_ref[...] = project().astype(o_ref.dtype)


def _in_proj(x, nw, w_main, w_gb, alog, dtb, cos_t, sin_a, sin_b):
    B, S, D = x.shape
    tm = min(1024, S)
    tn = IN_TILE_COLS
    tab = pl.BlockSpec((tm, D_HEAD), lambda b, i, j: (i, 0))
    return pl.pallas_call(
        _in_proj_kernel,
        out_shape=(jax.ShapeDtypeStruct((B, S, N_MAIN), BF16),
                   jax.ShapeDtypeStruct((B, S, LANES), F32),
                   jax.ShapeDtypeStruct((B, N_HEADS, VT_ROWS, S), BF16)),
        grid=(B, S // tm, N_MAIN // tn),
        in_specs=[
            pl.BlockSpec((None, tm, D), lambda b, i, j: (b, i, 0)),
            pl.BlockSpec((1, D), lambda b, i, j: (0, 0)),
            pl.BlockSpec((D, tn), lambda b, i, j: (0, j)),
            pl.BlockSpec((D, LANES), lambda b, i, j: (0, 0)),
            pl.BlockSpec((1, LANES), lambda b, i, j: (0, 0)),
            pl.BlockSpec((1, LANES), lambda b, i, j: (0, 0)),
            tab, tab, tab,
        ],
        out_specs=(pl.BlockSpec((None, tm, tn), lambda b, i, j: (b, i, j)),
                   pl.BlockSpec((None, tm, LANES), lambda b, i, j: (b, i, 0)),
                   pl.BlockSpec((None, N_HEADS, VT_ROWS, tm), lambda b, i, j: (b, 0, 0, i))),
        scratch_shapes=[pltpu.VMEM((tm, D), BF16)],
        compiler_params=_cparams("parallel", "parallel", "arbitrary"),
        name="in_proj",
    )(x, nw, w_main, w_gb, alog, dtb, cos_t, sin_a, sin_b)


def _kv_proj_kernel(x_ref, nw_ref, w_ref, o_ref):
    x = x_ref[...]
    ms = jnp.mean(x * x, axis=-1, keepdims=True)
    h = (x * lax.rsqrt(ms + EPS) * nw_ref[...]).astype(BF16)
    kv = jnp.dot(h, w_ref[...], preferred_element_type=F32)
    n_k = kv.shape[1] // 2
    o_ref[:, :n_k] = (kv[:, :n_k] * (D_HEAD ** -0.5 * math.log2(math.e))).astype(o_ref.dtype)
    o_ref[:, n_k:] = kv[:, n_k:].astype(o_ref.dtype)


def _kv_proj(mem, nw, w_kv):
    B, M, D = mem.shape
    n_out = w_kv.shape[1]
    return pl.pallas_call(
        _kv_proj_kernel,
        out_shape=jax.ShapeDtypeStruct((B, M, n_out), BF16),
        grid=(B,),
        in_specs=[
            pl.BlockSpec((None, M, D), lambda b: (b, 0, 0)),
            pl.BlockSpec((1, D), lambda b: (0, 0)),
            pl.BlockSpec((D, n_out), lambda b: (0, 0)),
        ],
        out_specs=pl.BlockSpec((None, M, n_out), lambda b: (b, 0, 0)),
        compiler_params=_cparams("parallel"),
        name="kv_proj",
    )(mem, nw, w_kv)


def _gdn_prep_kernel(xp_ref, x_ref, xn_ref, gb_ref, cw_ref, cm_ref, sh_ref, qkv_ref, col_ref):
    i = pl.program_id(1)
    n_i = pl.num_programs(1)
    T = x_ref.shape[0]
    pad = GDN_CONV // 2
    prev = jnp.where(i > 0, xp_ref[...], jnp.zeros_like(xp_ref[...]))
    nxt = jnp.where(i < n_i - 1, xn_ref[...], jnp.zeros_like(xn_ref[...]))
    xe = jnp.concatenate([prev, x_ref[...], nxt], axis=0)
    blk = min(CONV_ROWS, T)
    for r0 in range(0, T, blk):
        win = xe[r0:r0 + blk + 2 * HALO, :]
        taps = jnp.dot(sh_ref[...], win, preferred_element_type=F32)
        acc = xe[r0 + HALO:r0 + HALO + blk, :].astype(F32) * cw_ref[pad:pad + 1, :]
        for n, d in enumerate(dd for dd in range(GDN_CONV) if dd != pad):
            acc = acc + taps[n * blk:(n + 1) * blk, :] * cw_ref[d:d + 1, :]
        y = _silu(acc)
        rows = slice(r0, r0 + blk)
        for h in range(2 * N_HEADS):
            sl = slice(h * D_HEAD, (h + 1) * D_HEAD)
            t = y[:, sl]
            r = lax.rsqrt(jnp.sum(t * t, axis=-1, keepdims=True) + EPS)
            if h < N_HEADS:
                r = r * (D_HEAD ** -0.5)
            qkv_ref[rows, sl] = t * r
        qkv_ref[rows, 2 * W_BR:] = y[:, 2 * W_BR:]

        gb = gb_ref[rows, :]
        lane = lax.broadcasted_iota(jnp.int32, gb.shape, 1)
        hi, mid, lo = (p.astype(F32) for p in _split3(jnp.where(lane < GB_LANES, gb, 0.0)))
        packed = hi + pltpu.roll(mid, GB_LANES, 1) + pltpu.roll(lo, 2 * GB_LANES, 1)
        sums = jnp.dot(cm_ref[...], packed.astype(BF16), preferred_element_type=F32)
        sums = (sums + pltpu.roll(sums, LANES - GB_LANES, 1)
                + pltpu.roll(sums, LANES - 2 * GB_LANES, 1))
        col_ref[rows, :] = jnp.where(lane < 4, sums[:blk],
                                     jnp.where(lane < 8, sums[blk:2 * blk],
                                               jnp.where(lane < 16, gb, sums[2 * blk:])))


def _chunk_sum_matrices(T):
    r_i = np.arange(T)[:, None]
    c_i = np.arange(T)[None, :]
    same = (r_i // GDN_CHUNK) == (c_i // GDN_CHUNK)
    mats = np.concatenate([same & (c_i <= r_i), same & (c_i >= r_i), same], axis=0)
    return jnp.asarray(mats.astype(np.float32), dtype=BF16)


def _conv_shift_matrix(blk):
    pad = GDN_CONV // 2
    r_i = np.arange(blk)[:, None]
    c_i = np.arange(blk + 2 * HALO)[None, :]
    mats = np.concatenate([c_i == r_i + HALO + d - pad for d in range(GDN_CONV) if d != pad], axis=0)
    return jnp.asarray(mats.astype(np.float32), dtype=BF16)


def _gdn_prep(proj, gb, conv_w):
    B, S, _ = proj.shape
    T = min(1024, S)
    nb = T // HALO
    n_halo = S // HALO
    wq = 3 * W_BR
    blk = min(CONV_ROWS, T)
    return pl.pallas_call(
        _gdn_prep_kernel,
        out_shape=(jax.ShapeDtypeStruct((B, S, wq), F32),
                   jax.ShapeDtypeStruct((B, S, LANES), F32)),
        grid=(B, S // T),
        in_specs=[
            pl.BlockSpec((None, HALO, wq), lambda b, i: (b, jnp.maximum(i * nb - 1, 0), 0)),
            pl.BlockSpec((None, T, wq), lambda b, i: (b, i, 0)),
            pl.BlockSpec((None, HALO, wq), lambda b, i: (b, jnp.minimum((i + 1) * nb, n_halo - 1), 0)),
            pl.BlockSpec((None, T, LANES), lambda b, i: (b, i, 0)),
            pl.BlockSpec((GDN_CONV, wq), lambda b, i: (0, 0)),
            pl.BlockSpec((3 * blk, blk), lambda b, i: (0, 0)),
            pl.BlockSpec(((GDN_CONV - 1) * blk, blk + 2 * HALO), lambda b, i: (0, 0)),
        ],
        out_specs=(pl.BlockSpec((None, T, wq), lambda b, i: (b, i, 0)),
                   pl.BlockSpec((None, T, LANES), lambda b, i: (b, i, 0))),
        compiler_params=_cparams("parallel", "parallel"),
        name="gdn_prep",
    )(proj, proj, proj, gb, conv_w, _chunk_sum_matrices(blk), _conv_shift_matrix(blk))


def _gdn_masks(reverse):
    n = N_HEADS * GDN_CHUNK
    r_i = lax.broadcasted_iota(jnp.int32, (n, n), 0)
    c_i = lax.broadcasted_iota(jnp.int32, (n, n), 1)
    same_head = _same_block(r_i, c_i, GDN_CHUNK)
    same_blk = _same_block(r_i, c_i, 8)
    if reverse:
        incl = same_head & (c_i >= r_i)
        strict = same_head & (c_i > r_i)
    else:
        incl = same_head & (c_i <= r_i)
        strict = same_head & (c_i < r_i)
    return incl, strict, same_blk


def _stack_heads(x):
    return jnp.concatenate([x[:, h * D_HEAD:(h + 1) * D_HEAD] for h in range(N_HEADS)], axis=0)


def _mm(a, b):
    return jnp.dot(a, b, preferred_element_type=F32)


def _bf(xs):
    return [x.astype(BF16) for x in xs]


def _unit_tri_solve(a_mats, rhss, same_blks):
    n = a_mats[0].shape[0]
    eye = jnp.where(lax.broadcasted_iota(jnp.int32, (n, n), 0)
                    == lax.broadcasted_iota(jnp.int32, (n, n), 1), 1.0, 0.0)
    d1 = [jnp.where(m, a, 0.0) for a, m in zip(a_mats, same_blks)]
    lo_rhs = _bf([jnp.concatenate([a - d, r], axis=1) for a, d, r in zip(a_mats, d1, rhss)])
    d1b = _bf(d1)
    d2 = [_mm(d, d) for d in d1b]
    d2b = _bf(d2)
    d4b = _bf([_mm(d, d) for d in d2b])
    d3 = [_mm(a, b) for a, b in zip(d1b, d2b)]
    p1 = [eye - a + b - c for a, b, c in zip(d1, d2, d3)]
    t8b = _bf([p + _mm(p.astype(BF16), d) for p, d in zip(p1, d4b)])
    by = [_mm(t, w) for t, w in zip(t8b, lo_rhs)]
    b1b = _bf([x[:, :n] for x in by])
    b2b = _bf([_mm(b, b) for b in b1b])
    b4b = _bf([_mm(b, b) for b in b2b])
    z = [x[:, n:] for x in by]
    for bb in (b1b, b2b, b4b):
        sign = -1.0 if bb is b1b else 1.0
        z = [a + sign * _mm(b, a.astype(BF16)) for a, b in zip(z, bb)]
    return z


def _gdn_prepare(q, k, v, col, row, tot, d):
    g_idx = d * N_HEADS
    b_idx = 8 + d * N_HEADS
    t_idx = 16 + d * N_HEADS
    qs, kbs, ks, rhs_v, rhs_k, gcs, kds, e_tot = [], [], [], [], [], [], [], []
    for h in range(N_HEADS):
        sl = slice(h * D_HEAD, (h + 1) * D_HEAD)
        gc = col[:, g_idx + h:g_idx + h + 1]
        bc = col[:, b_idx + h:b_idx + h + 1]
        gt = col[:, t_idx + h:t_idx + h + 1]
        eg = jnp.exp(gc)
        k_h = k[:, sl]
        kb = k_h * bc
        qs.append(q[:, sl] * eg)
        kbs.append(kb)
        ks.append(k_h)
        rhs_v.append(v[:, sl] * bc)
        rhs_k.append(kb * eg)
        gcs.append(jnp.broadcast_to(gc, (GDN_CHUNK, N_HEADS * GDN_CHUNK)))
        kds.append((k_h * jnp.exp(gt - gc)).astype(BF16))
        e_tot.append(jnp.exp(tot[g_idx + h:g_idx + h + 1, :]))
    return dict(
        gram_lhs=jnp.concatenate(kbs + [_stack_heads(q)], axis=0).astype(BF16),
        k_st=jnp.concatenate(ks, axis=0).astype(BF16),
        dlog=jnp.concatenate(gcs, axis=0) - row[d:d + 1, :],
        rhs=jnp.concatenate([jnp.concatenate(rhs_v, axis=0), jnp.concatenate(rhs_k, axis=0)], axis=1),
        qs=jnp.concatenate(qs, axis=0), kds=kds, e_tot=e_tot)


def _gdn_affine_terms(chains, masks):
    n = N_HEADS * GDN_CHUNK
    gram = [lax.dot_general(c["gram_lhs"], c["k_st"], (((1,), (1,)), ((), ())),
                            preferred_element_type=F32) for c in chains]
    decay = [jnp.exp(jnp.where(m[0], c["dlog"], NEG)) for c, m in zip(chains, masks)]
    a_mats = [jnp.where(m[1], g[:n] * dc, 0.0) for g, dc, m in zip(gram, decay, masks)]
    attn_b = _bf([g[n:] * dc for g, dc in zip(gram, decay)])
    xb = _bf(_unit_tri_solve(a_mats, [c["rhs"] for c in chains], [m[2] for m in masks]))
    ax = [_mm(a, x) for a, x in zip(attn_b, xb)]
    out = []
    for c, x, a in zip(chains, xb, ax):
        heads = []
        for h in range(N_HEADS):
            rs = slice(h * GDN_CHUNK, (h + 1) * GDN_CHUNK)
            kx = lax.dot_general(c["kds"][h], x[rs], (((0,), (0,)), ((), ())),
                                 preferred_element_type=F32)
            p = c["qs"][rs] - a[rs, D_HEAD:]
            heads.append(dict(lhs=jnp.concatenate([kx[:, D_HEAD:], p], axis=0).astype(BF16),
                              n=kx[:, :D_HEAD], r=a[rs, :D_HEAD], e=c["e_tot"][h]))
        out.append(heads)
    return out


def _gdn_kernel(qf_ref, kf_ref, vf_ref, colf_ref, rowf_ref, totf_ref,
                qb_ref, kb_ref, vb_ref, colb_ref, rowb_ref, totb_ref,
                of_ref, ob_ref, s_ref):
    nc = qf_ref.shape[0]

    @pl.when(pl.program_id(1) == 0)
    def _():
        s_ref[...] = jnp.zeros(s_ref.shape, F32)

    masks_f = _gdn_masks(False)
    masks_b = _gdn_masks(True)
    chains, masks = [], []
    for c in range(nc):
        cb = nc - 1 - c
        chains.append(_gdn_prepare(qf_ref[c], kf_ref[c], vf_ref[c], colf_ref[c], rowf_ref[c],
                                   totf_ref[c], 0))
        chains.append(_gdn_prepare(qb_ref[cb], kb_ref[cb], vb_ref[cb], colb_ref[cb], rowb_ref[cb],
                                   totb_ref[cb], 1))
        masks += [masks_f, masks_b]
    terms = _gdn_affine_terms(chains, masks)

    state = [[s_ref[d, h] for h in range(N_HEADS)] for d in range(2)]
    for c in range(nc):
        ls = [[_mm(terms[2 * c + d][h]["lhs"], state[d][h].astype(BF16)) for h in range(N_HEADS)]
              for d in range(2)]
        for d, o_ref, idx in ((0, of_ref, c), (1, ob_ref, nc - 1 - c)):
            outs = []
            for h in range(N_HEADS):
                t = terms[2 * c + d][h]
                state[d][h] = state[d][h] * t["e"] - ls[d][h][:D_HEAD] + t["n"]
                outs.append(ls[d][h][D_HEAD:] + t["r"])
            o_ref[idx] = jnp.concatenate(outs, axis=1)
    for d in range(2):
        for h in range(N_HEADS):
            s_ref[d, h] = state[d][h]


def _gdn(qkv, col, row, tot):
    B, N, C, _ = qkv.shape
    nc = min(8, N)
    nblk = N // nc
    fwd = lambda j: (lambda b, i: (b, i, 0, j))
    bwd = lambda j: (lambda b, i: (b, nblk - 1 - i, 0, j))
    specs = []
    for mk in (fwd, bwd):
        specs += [pl.BlockSpec((None, nc, C, W_BR), mk(0)),
                  pl.BlockSpec((None, nc, C, W_BR), mk(1)),
                  pl.BlockSpec((None, nc, C, W_BR), mk(2)),
                  pl.BlockSpec((None, nc, C, LANES), mk(0)),
                  pl.BlockSpec((None, nc, 2, N_HEADS * C), mk(0)),
                  pl.BlockSpec((None, nc, 2 * N_HEADS, LANES), mk(0))]
    out_sds = jax.ShapeDtypeStruct((B, N, C, W_BR), F32)
    return pl.pallas_call(
        _gdn_kernel,
        out_shape=(out_sds, out_sds),
        grid=(B, nblk),
        in_specs=specs,
        out_specs=(pl.BlockSpec((None, nc, C, W_BR), fwd(0)),
                   pl.BlockSpec((None, nc, C, W_BR), bwd(0))),
        scratch_shapes=[pltpu.VMEM((2, N_HEADS, D_HEAD, D_HEAD), F32)],
        compiler_params=_cparams("parallel", "arbitrary"),
        name="gdn",
    )(qkv, qkv, qkv, col, row, tot, qkv, qkv, qkv, col, row, tot)


def _diff_attn_kernel(q_ref, k_ref, vt_ref, z_ref, lp_ref, nw_ref, y_ref, *, lambda_init):
    tq = q_ref.shape[0]
    sub = min(DIFF_SUB, tq)
    n_keys = k_ref.shape[0]
    ck = min(DIFF_KEYS, n_keys)
    lane = lax.broadcasted_iota(jnp.int32, (sub, D_HEAD), 1)
    lp = lp_ref[...]
    lam = (jnp.exp(jnp.sum(lp[0:1] * lp[1:2], axis=-1, keepdims=True))
           - jnp.exp(jnp.sum(lp[2:3] * lp[3:4], axis=-1, keepdims=True)) + lambda_init)
    streams = [(j, first) for j in range(tq // sub) for first in (True, False)]
    q_masked = []
    for j, first in streams:
        q = q_ref[j * sub:(j + 1) * sub, :]
        q_masked.append(jnp.where((lane < DQK) == first, q, jnp.zeros_like(q)))

    def scores(c, t):
        k_c = k_ref[c * ck:(c + 1) * ck, :]
        return lax.dot_general(k_c, q_masked[t], (((1,), (1,)), ((), ())),
                               preferred_element_type=F32)

    run_max = [None] * len(streams)
    acc = [None] * len(streams)
    pending = [scores(0, t) for t in range(len(streams))]
    for c in range(n_keys // ck):
        vt_c = vt_ref[:, c * ck:(c + 1) * ck]
        for t in range(len(streams)):
            s = pending[t]
            if (c + 1) * ck < n_keys:
                pending[t] = scores(c + 1, t)
            m_c = jnp.max(s, axis=0, keepdims=True)
            m_new = m_c if c == 0 else jnp.maximum(run_max[t], m_c)
            p = jnp.exp2(s - m_new).astype(BF16)
            pv = jnp.dot(vt_c, p, preferred_element_type=F32)
            acc[t] = pv if c == 0 else acc[t] * jnp.exp2(run_max[t] - m_new) + pv
            run_max[t] = m_new

    for j in range(tq // sub):
        halves = []
        for first in (True, False):
            o_aug = acc[streams.index((j, first))]
            inv = 1.0 / o_aug[D_HEAD:D_HEAD + 1, :]
            halves.append(o_aug[:D_HEAD, :] * (inv if first else inv * lam))
        o = (halves[0] - halves[1]).T
        r = lax.rsqrt(jnp.mean(o * o, axis=-1, keepdims=True) + EPS)
        o = o * r * nw_ref[...] * (1.0 - lambda_init)
        rows = slice(j * sub, (j + 1) * sub)
        y_ref[rows, :] = (o * _silu(z_ref[rows, :].astype(F32))).astype(y_ref.dtype)


def _diff_attn(proj, v_t, lam_p, nw, lambda_init):
    B, S, _ = proj.shape
    tq = min(8 * DIFF_SUB, S)
    nh = W_BR // D_HEAD
    return pl.pallas_call(
        functools.partial(_diff_attn_kernel, lambda_init=lambda_init),
        out_shape=jax.ShapeDtypeStruct((B, S, W_BR), BF16),
        grid=(B, N_HEADS, S // tq),
        in_specs=[
            pl.BlockSpec((None, tq, D_HEAD), lambda b, h, i: (b, i, COL_B_Q * nh + h)),
            pl.BlockSpec((None, S, D_HEAD), lambda b, h, i: (b, 0, COL_B_K * nh + h)),
            pl.BlockSpec((None, None, VT_ROWS, S), lambda b, h, i: (b, h, 0, 0)),
            pl.BlockSpec((None, tq, D_HEAD), lambda b, h, i: (b, i, COL_B_Z * nh + h)),
            pl.BlockSpec((4, DQK), lambda b, h, i: (0, 0)),
            pl.BlockSpec((1, D_HEAD), lambda b, h, i: (0, 0)),
        ],
        out_specs=pl.BlockSpec((None, tq, D_HEAD), lambda b, h, i: (b, i, h)),
        compiler_params=_cparams("parallel", "parallel", "parallel"),
        name="diff_attn",
    )(proj, proj, v_t, proj, lam_p, nw)


def _conv_mem_kernel(cb_ref, cc_ref, cx_ref, cz_ref, ccp_ref, cxp_ref, ccn_ref, cxn_ref,
                     mq_ref, mz_ref, kv_ref, cw_ref, y_ref):
    i = pl.program_id(1)
    n_i = pl.num_programs(1)
    T = cb_ref.shape[0]
    prev = jnp.where(i > 0, ccp_ref[HALO - 8:, :].astype(F32) * cxp_ref[HALO - 8:, :].astype(F32), 0.0)
    nxt = jnp.where(i < n_i - 1, ccn_ref[:8, :].astype(F32) * cxn_ref[:8, :].astype(F32), 0.0)
    cur = cc_ref[...].astype(F32) * cx_ref[...].astype(F32)
    xe = jnp.concatenate([prev, cur, nxt], axis=0)
    pad = CONV_W // 2
    acc = xe[8 - pad:8 - pad + T, :] * cw_ref[0:1, :]
    for d in range(1, CONV_W):
        acc = acc + xe[8 - pad + d:8 - pad + d + T, :] * cw_ref[d:d + 1, :]
    y_c = cb_ref[...].astype(F32) * acc * _silu(cz_ref[...].astype(F32))
    y_ref[:, :W_BR] = y_c.astype(y_ref.dtype)

    mz = mz_ref[...].astype(F32)
    ones = jnp.ones((kv_ref.shape[0], D_HEAD), kv_ref.dtype)
    for h in range(N_HEADS):
        sl = slice(h * D_HEAD, (h + 1) * D_HEAD)
        s = lax.dot_general(mq_ref[:, sl], kv_ref[:, sl], (((1,), (1,)), ((), ())),
                            preferred_element_type=F32)
        p = jnp.exp2(s - jnp.max(s, axis=-1, keepdims=True)).astype(BF16)
        v_aug = jnp.concatenate([kv_ref[:, W_BR + h * D_HEAD:W_BR + (h + 1) * D_HEAD], ones], axis=1)
        o = jnp.dot(p, v_aug, preferred_element_type=F32)
        o = o[:, :D_HEAD] * (1.0 / o[:, D_HEAD:])
        y_ref[:, W_BR + h * D_HEAD:W_BR + (h + 1) * D_HEAD] = (o * _silu(mz[:, sl])).astype(y_ref.dtype)


def _conv_mem(proj, kv, conv_w):
    B, S, _ = proj.shape
    M = kv.shape[1]
    T = min(1024, S)
    nb = T // HALO
    n_halo = S // HALO
    main = lambda c: pl.BlockSpec((None, T, W_BR), lambda b, i: (b, i, c))
    prev = lambda c: pl.BlockSpec((None, HALO, W_BR), lambda b, i: (b, jnp.maximum(i * nb - 1, 0), c))
    nxt = lambda c: pl.BlockSpec((None, HALO, W_BR),
                                 lambda b, i: (b, jnp.minimum((i + 1) * nb, n_halo - 1), c))
    return pl.pallas_call(
        _conv_mem_kernel,
        out_shape=jax.ShapeDtypeStruct((B, S, 2 * W_BR), BF16),
        grid=(B, S // T),
        in_specs=[main(COL_C_B), main(COL_C_C), main(COL_C_X), main(COL_C_Z),
                  prev(COL_C_C), prev(COL_C_X), nxt(COL_C_C), nxt(COL_C_X),
                  main(COL_M_Q), main(COL_M_Z),
                  pl.BlockSpec((None, M, 2 * W_BR), lambda b, i: (b, 0, 0)),
                  pl.BlockSpec((CONV_W, W_BR), lambda b, i: (0, 0))],
        out_specs=pl.BlockSpec((None, T, 2 * W_BR), lambda b, i: (b, i, 0)),
        compiler_params=_cparams("parallel", "parallel"),
        name="conv_mem",
    )(proj, proj, proj, proj, proj, proj, proj, proj, proj, proj, kv, conv_w)


def _out_proj_kernel(of_ref, ob_ref, az_ref, gnw_ref, yb_ref, ycm_ref, w_ref, x_ref, nw_ref, o_ref):
    o = of_ref[...] + ob_ref[...]
    z = az_ref[...].astype(F32)
    heads = []
    for h in range(N_HEADS):
        sl = slice(h * D_HEAD, (h + 1) * D_HEAD)
        t = o[:, sl]
        r = lax.rsqrt(jnp.mean(t * t, axis=-1, keepdims=True) + EPS)
        heads.append((t * r * gnw_ref[...] * _silu(z[:, sl])).astype(BF16))
    y = jnp.dot(jnp.concatenate([yb_ref[...], ycm_ref[...]] + heads, axis=1), w_ref[...],
                preferred_element_type=F32)
    r = lax.rsqrt(jnp.mean(y * y, axis=-1, keepdims=True) + EPS)
    o_ref[...] = x_ref[...] + y * r * nw_ref[...]


def _out_proj(o_f, o_b, proj, gdn_nw, y_b, y_cm, w_out, x, nw):
    B, S, D = x.shape
    T = min(512, S)
    return pl.pallas_call(
        _out_proj_kernel,
        out_shape=jax.ShapeDtypeStruct((B, S, D), F32),
        grid=(B, S // T),
        in_specs=[
            pl.BlockSpec((None, T, W_BR), lambda b, i: (b, i, 0)),
            pl.BlockSpec((None, T, W_BR), lambda b, i: (b, i, 0)),
            pl.BlockSpec((None, T, W_BR), lambda b, i: (b, i, COL_A_Z)),
            pl.BlockSpec((1, D_HEAD), lambda b, i: (0, 0)),
            pl.BlockSpec((None, T, W_BR), lambda b, i: (b, i, 0)),
            pl.BlockSpec((None, T, 2 * W_BR), lambda b, i: (b, i, 0)),
            pl.BlockSpec((4 * W_BR, D), lambda b, i: (0, 0)),
            pl.BlockSpec((None, T, D), lambda b, i: (b, i, 0)),
            pl.BlockSpec((1, D), lambda b, i: (0, 0)),
        ],
        out_specs=pl.BlockSpec((None, T, D), lambda b, i: (b, i, 0)),
        compiler_params=_cparams("parallel", "parallel"),
        name="out_proj",
    )(o_f, o_b, proj, gdn_nw, y_b, y_cm, w_out, x, nw)


def _rope_tables(S):
    half = ROT_DIM // 2
    f32 = np.float32
    inv = (f32(ROPE_THETA) ** (-np.arange(0, ROT_DIM, 2, dtype=f32) / f32(ROT_DIM))).astype(f32)
    ang = np.arange(S, dtype=f32)[:, None] * inv[None, :]
    cos, sin = np.cos(ang).astype(f32), np.sin(ang).astype(f32)
    ones = np.ones((S, DQK - ROT_DIM), f32)
    zeros = np.zeros((S, DQK - ROT_DIM), f32)
    z8 = np.zeros((S, half), f32)
    c_map = np.concatenate([cos, cos, ones], axis=1)
    sa_map = np.concatenate([z8, sin, zeros], axis=1)
    sb_map = np.concatenate([-sin, z8, zeros], axis=1)
    two = lambda t: jnp.asarray(np.concatenate([t, t], axis=1))
    return two(c_map), two(sa_map), two(sb_map)


def _row_info(col, N):
    B = col.shape[0]
    c4 = col.reshape(B, N, GDN_CHUNK, LANES)
    g_st = jnp.swapaxes(c4[..., 0:8], 2, 3).reshape(B, N, 2, N_HEADS * GDN_CHUNK)
    tot = jnp.broadcast_to(c4[:, :, 0, 16:24][..., None], (B, N, 2 * N_HEADS, LANES))
    return g_st, tot


def _prep_layer_weights(l, norm_pre, norm_post, norm_mem, w_in, gdn_conv, gdn_A_log, gdn_dt_bias,
                        gdn_norm, diff_lambda, diff_norm, conv_w, w_mem_kv, w_out):
    w = w_in[l]
    q_end = 3 * W_BR
    n_db = 2 * N_HEADS
    dec = w[:, q_end:q_end + n_db]
    bet = w[:, q_end + n_db:q_end + 2 * n_db]
    w_main = jnp.concatenate([w[:, :q_end], w[:, q_end + 2 * n_db:]], axis=1).astype(BF16)
    w_gb = jnp.concatenate([dec, bet, dec, jnp.zeros((w.shape[0], LANES - 3 * n_db), F32)],
                           axis=1).astype(BF16)
    lane_pad = lambda t: jnp.concatenate(
        [t.reshape(1, n_db), jnp.zeros((1, n_db), F32), t.reshape(1, n_db),
         jnp.zeros((1, LANES - 3 * n_db), F32)], axis=1)
    return dict(
        norm_pre=norm_pre[l][None, :], norm_post=norm_post[l][None, :], norm_mem=norm_mem[l][None, :],
        w_main=w_main, w_gb=w_gb, alog=lane_pad(gdn_A_log[l]), dtb=lane_pad(gdn_dt_bias[l]),
        gdn_conv=gdn_conv[l], gdn_norm=gdn_norm[l][None, :], diff_lambda=diff_lambda[l],
        diff_norm=diff_norm[l][None, :], conv_w=conv_w[l], w_kv=w_mem_kv[l].astype(BF16),
        w_out=jnp.concatenate([w_out[l][W_BR:], w_out[l][:W_BR]], axis=0).astype(BF16))


def _layer(x, mem, p, lambda_init, tables):
    B, S, _ = x.shape
    N = S // GDN_CHUNK
    proj, gb, v_t = _in_proj(x, p["norm_pre"], p["w_main"], p["w_gb"], p["alog"], p["dtb"], *tables)
    kv = _kv_proj(mem, p["norm_mem"], p["w_kv"])
    qkv, col = _gdn_prep(proj, gb, p["gdn_conv"])
    row, tot = _row_info(col, N)
    o_f, o_b = _gdn(qkv.reshape(B, N, GDN_CHUNK, 3 * W_BR), col.reshape(B, N, GDN_CHUNK, LANES),
                    row, tot)
    y_b = _diff_attn(proj, v_t, p["diff_lambda"], p["diff_norm"], lambda_init)
    y_cm = _conv_mem(proj, kv, p["conv_w"])
    return _out_proj(o_f.reshape(B, S, W_BR), o_b.reshape(B, S, W_BR), proj, p["gdn_norm"],
                     y_b, y_cm, p["w_out"], x, p["norm_post"])


def _trunk(x, mem, layer_params):
    tables = _rope_tables(x.shape[1])
    for l, p in enumerate(layer_params):
        lambda_init = 0.8 - 0.6 * math.exp(-0.3 * l)
        x = _layer(x, mem, p, lambda_init, tables)
    return x


def kernel(x_prompt, x_sample, mem_prompt, mem_sample, norm_pre, norm_post, norm_mem, w_in, gdn_conv,
           gdn_A_log, gdn_dt_bias, gdn_norm, diff_lambda, diff_norm, conv_w, w_mem_kv, w_out):
    depth = w_in.shape[0]
    params = [_prep_layer_weights(l, norm_pre, norm_post, norm_mem, w_in, gdn_conv, gdn_A_log,
                                  gdn_dt_bias, gdn_norm, diff_lambda, diff_norm, conv_w, w_mem_kv, w_out)
              for l in range(depth)]
    return (_trunk(x_prompt, mem_prompt, params), _trunk(x_sample, mem_sample, params))
```
